```python
import math, functools
import jax, jax.numpy as jnp
from jax import lax
import numpy as np

D_MODEL = 1024
BATCH = 2
SEQ = 8192
DEPTH = 2
DEC_BATCH = 32
DEC_SEQ = 1
PAST_LEN = 16384
PAGE_SIZE = 128

CONV_CHANNELS = D_MODEL // 2
CONV_WIDTH = 3
POOL_WINDOWS = (2, 4, 8, 16)
POOL_GROUP = D_MODEL // 8
POOL_WIDTH = POOL_GROUP * len(POOL_WINDOWS)
POOL_STATE = max(POOL_WINDOWS) - 1
N_HEADS = 8
HEAD_DIM = 64
ATTN_WIDTH = N_HEADS * HEAD_DIM
N_IDX_HEADS = 8
IDX_DIM = 64
TOPK_MAX = 256
Q_BLOCK = 128
NUM_BUCKETS = 32
MAX_DISTANCE = 128
N_EXPERTS = 32
TOP_K = 4
D_FF = D_MODEL
SWIGLU_LIMIT = 7.0
SWIGLU_ALPHA = 1.702
MOE_BLOCK = 128
LN_EPS = 1e-5
DEEPNORM_ALPHA = (2 * DEPTH) ** 0.25
DEEPNORM_BETA = (8 * DEPTH) ** -0.25
IN_SIZES = (CONV_CHANNELS, CONV_CHANNELS, CONV_CHANNELS, POOL_WIDTH, ATTN_WIDTH, ATTN_WIDTH, ATTN_WIDTH,
            N_IDX_HEADS * IDX_DIM, IDX_DIM, N_IDX_HEADS, 3 * D_MODEL)
N_IN = sum(IN_SIZES)

kernel_name = "hybrid_conv_pool_dsa_moe_deepnorm_step"


def layer_norm(x, g, b):
    xf = x.astype(jnp.float32)
    mu = jnp.mean(xf, axis=-1, keepdims=True)
    var = jnp.mean(jnp.square(xf - mu), axis=-1, keepdims=True)
    return ((xf - mu) * lax.rsqrt(var + LN_EPS) * g.astype(jnp.float32) + b.astype(jnp.float32)).astype(x.dtype)


def t5_bucket(dist):
    exact = NUM_BUCKETS // 2
    d = jnp.maximum(dist, 1).astype(jnp.float32)
    large = exact + (jnp.log(d / exact) / math.log(MAX_DISTANCE / exact) * (NUM_BUCKETS - exact)).astype(jnp.int32)
    return jnp.where(dist < exact, dist, jnp.minimum(large, NUM_BUCKETS - 1))


def short_conv_branch(a_b, a_c, a_h, conv_prev, conv_w):
    u = a_c * a_h
    ext = jnp.concatenate([conv_prev.astype(u.dtype), u], axis=1)
    n_pos = u.shape[1]
    conv = sum(conv_w[j] * ext[:, j:j + n_pos] for j in range(CONV_WIDTH))
    return a_b * conv, ext[:, -(CONV_WIDTH - 1):]


def pool_branch(u, pool_prev, pos0, pool_w, pool_scale):
    ext = jnp.concatenate([pool_prev.astype(u.dtype), u], axis=1)
    n_b, n_pos, _ = u.shape
    cs = jnp.cumsum(ext.astype(jnp.float32), axis=1)
    cs = jnp.concatenate([jnp.zeros((n_b, 1, POOL_WIDTH), jnp.float32), cs], axis=1)
    pos = pos0 + jnp.arange(n_pos)
    outs = []
    for g, w in enumerate(POOL_WINDOWS):
        sl = slice(g * POOL_GROUP, (g + 1) * POOL_GROUP)
        win = cs[:, POOL_STATE + 1:POOL_STATE + 1 + n_pos, sl] - cs[:, POOL_STATE + 1 - w:POOL_STATE + 1 - w + n_pos, sl]
        cnt = jnp.minimum(w, pos + 1).astype(jnp.float32)[None, :, None]
        diff = win / cnt - u[..., sl].astype(jnp.float32)
        outs.append(jnp.einsum('blc,cd->bld', diff, pool_w[g].astype(jnp.float32)))
    y = jnp.concatenate(outs, axis=-1).astype(u.dtype) * pool_scale
    return y, ext[:, -POOL_STATE:]


def indexer_scores(q_idx, w_idx, k_idx, tpos):
    dots = jnp.einsum('bqhc,bsc->bqhs', q_idx, k_idx).astype(jnp.float32) * (IDX_DIM ** -0.5)
    score = jnp.einsum('bqh,bqhs->bqs', w_idx.astype(jnp.float32) * (N_IDX_HEADS ** -0.5), jax.nn.relu(dots))
    adm = jnp.arange(k_idx.shape[1])[None, :] <= tpos[:, None]
    return jnp.where(adm[None], score, -jnp.inf)


def attend_selected(q, k_sel, v_sel, sel, tpos, rel_bias):
    dist = tpos[None, :, None] - sel
    valid = dist >= 0
    bias = rel_bias[t5_bucket(jnp.maximum(dist, 0))]
    logits = jnp.einsum('bqhd,bqkhd->bqhk', q, k_sel).astype(jnp.float32) * (HEAD_DIM ** -0.5)
    logits = logits + jnp.moveaxis(bias, -1, 2).astype(jnp.float32)
    logits = jnp.where(valid[:, :, None, :], logits, -jnp.inf)
    p = jax.nn.softmax(logits, axis=-1).astype(v_sel.dtype)
    return jnp.einsum('bqhk,bqkhd->bqhd', p, v_sel)


def dsa_prompt(q, k, v, q_idx, k_idx, w_idx, *, rel_bias):
    n_b, n_pos = q.shape[:2]
    n_sel = min(TOPK_MAX, n_pos // 4)
    n_blk = n_pos // Q_BLOCK
    take = jax.vmap(lambda a, s: a[s])

    def to_blocks(a):
        return jnp.moveaxis(a.reshape((n_b, n_blk, Q_BLOCK) + a.shape[2:]), 1, 0)

    def block(args):
        qb, qib, wb, start = args
        tpos = start + jnp.arange(Q_BLOCK)
        score = indexer_scores(qib, wb, k_idx, tpos)
        _, sel = lax.top_k(score, n_sel)
        return attend_selected(qb, take(k, sel), take(v, sel), sel, tpos, rel_bias)

    out = lax.map(block, (to_blocks(q), to_blocks(q_idx), to_blocks(w_idx), jnp.arange(n_blk) * Q_BLOCK))
    return jnp.moveaxis(out, 0, 1).reshape(q.shape)


def dsa_sample(q, k, v, q_idx, k_idx, w_idx, *, layer, cache_k, cache_v, cache_kidx, page_table, rel_bias):
    n_b, n_new = q.shape[:2]
    past = page_table.shape[1] * PAGE_SIZE
    n_sel = min(TOPK_MAX, (past + n_new) // 4)
    tpos = past + jnp.arange(n_new)
    kidx_past = cache_kidx[layer, page_table].reshape(n_b, past, IDX_DIM)
    kidx_all = jnp.concatenate([kidx_past.astype(k_idx.dtype), k_idx], axis=1)
    score = indexer_scores(q_idx, w_idx, kidx_all, tpos)
    _, sel = lax.top_k(score, n_sel)
    in_past = (sel < past)[..., None, None]
    s_past = jnp.minimum(sel, past - 1)
    phys = jax.vmap(lambda pt, s: pt[s])(page_table, s_past // PAGE_SIZE)
    row = s_past % PAGE_SIZE
    s_new = jnp.clip(sel - past, 0, n_new - 1)
    take = jax.vmap(lambda a, s: a[s])
    k_sel = jnp.where(in_past, cache_k[layer, phys, row].astype(k.dtype), take(k, s_new))
    v_sel = jnp.where(in_past, cache_v[layer, phys, row].astype(v.dtype), take(v, s_new))
    return attend_selected(q, k_sel, v_sel, sel, tpos, rel_bias)


def moe(x, w_router, b_router, w_gate, b_gate, w_up, b_up, w_down, b_down):
    n_tok = x.shape[0]
    logits = (x @ w_router + b_router).astype(jnp.float32)
    top_v, top_e = lax.top_k(logits, TOP_K)
    gate = jax.nn.softmax(top_v, axis=-1)
    n_assign = n_tok * TOP_K
    flat_e = top_e.reshape(-1)
    flat_t = jnp.arange(n_assign) // TOP_K
    flat_g = gate.reshape(-1)
    order = jnp.argsort(flat_e)
    se, st, sg = flat_e[order], flat_t[order], flat_g[order]
    counts = jnp.bincount(flat_e, length=N_EXPERTS)
    padded = (counts + MOE_BLOCK - 1) // MOE_BLOCK * MOE_BLOCK
    pad_end = jnp.cumsum(padded)
    pad_start = pad_end - padded
    grp_start = jnp.cumsum(counts) - counts
    dest = pad_start[se] + jnp.arange(n_assign) - grp_start[se]
    n_blocks = -(-n_assign // MOE_BLOCK) + N_EXPERTS
    n_rows = n_blocks * MOE_BLOCK
    row_t = jnp.full((n_rows,), n_tok, jnp.int32).at[dest].set(st)
    row_g = jnp.zeros((n_rows,), x.dtype).at[dest].set(sg.astype(x.dtype))
    blk_e = jnp.minimum(jnp.searchsorted(pad_end, jnp.arange(n_blocks) * MOE_BLOCK, side='right'), N_EXPERTS - 1)
    x_pad = jnp.concatenate([x, jnp.zeros((1, x.shape[1]), x.dtype)], axis=0)
    xb = x_pad[row_t].reshape(n_blocks, MOE_BLOCK, x.shape[1])

    def expert_block(args):
        xe, e = args
        g = jnp.minimum(xe @ w_gate[e] + b_gate[e], SWIGLU_LIMIT)
        up = jnp.clip(xe @ w_up[e] + b_up[e], -SWIGLU_LIMIT, SWIGLU_LIMIT)
        h = g * jax.nn.sigmoid(SWIGLU_ALPHA * g) * (up + 1.0)
        return h @ w_down[e] + b_down[e]

    yb = lax.map(expert_block, (xb, blk_e)).reshape(n_rows, -1) * row_g[:, None]
    return jax.ops.segment_sum(yb, row_t, num_segments=n_tok + 1)[:n_tok]


def layer(x, pos0, conv_prev, pool_prev, attn, w_in, b_in, conv_w, pool_w, pool_scale, p_a, p_b, p_c, w_o,
          ln1_g, ln1_b, w_router, b_router, w_gate, b_gate, w_up, b_up, w_down, b_down, ln2_g, ln2_b):
    n_b, n_pos, _ = x.shape
    z = jnp.einsum('bld,dn->bln', x, w_in) + b_in
    a_b, a_c, a_h, u_pool, q, k, v, q_idx, k_idx, w_idx, g_lin = jnp.split(
        z, np.cumsum(IN_SIZES)[:-1].tolist(), axis=-1)
    q = q.reshape(n_b, n_pos, N_HEADS, HEAD_DIM)
    k = k.reshape(n_b, n_pos, N_HEADS, HEAD_DIM)
    v = v.reshape(n_b, n_pos, N_HEADS, HEAD_DIM)
    q_idx = q_idx.reshape(n_b, n_pos, N_IDX_HEADS, IDX_DIM)
    y_a, conv_new = short_conv_branch(a_b, a_c, a_h, conv_prev, conv_w)
    y_b, pool_new = pool_branch(u_pool, pool_prev, pos0, pool_w, pool_scale)
    y_c = attn(q, k, v, q_idx, k_idx, w_idx).reshape(n_b, n_pos, ATTN_WIDTH)
    g_a, g_b, g_c = jnp.split(jax.nn.sigmoid(g_lin), 3, axis=-1)
    merged = g_a * (y_a @ p_a) + g_b * (y_b @ p_b) + g_c * (y_c @ p_c)
    x = layer_norm(DEEPNORM_ALPHA * x + merged @ w_o, ln1_g, ln1_b)
    h = moe(x.reshape(-1, D_MODEL), w_router, b_router, w_gate, b_gate, w_up, b_up, w_down, b_down)
    x = layer_norm(DEEPNORM_ALPHA * x + h.reshape(x.shape), ln2_g, ln2_b)
    return x, k, v, k_idx, conv_new, pool_new


def setup_inputs(seed: int = 0) -> dict:
    key = jax.random.key(seed)
    ks = iter(jax.random.split(key, 40))
    f32 = jnp.float32

    def nrm(shape, scale=1.0):
        return jax.random.normal(next(ks), shape, f32) * scale

    n_pages = PAST_LEN // PAGE_SIZE
    n_pool = (DEC_BATCH * n_pages * 5) // 4
    page_table = jax.random.permutation(next(ks), n_pool)[:DEC_BATCH * n_pages].reshape(DEC_BATCH, n_pages).astype(jnp.int32)
    beta = DEEPNORM_BETA
    return {
        "x_prompt": nrm((BATCH, SEQ, D_MODEL)),
        "x_sample": nrm((DEC_BATCH, DEC_SEQ, D_MODEL)),
        "cache_k": nrm((DEPTH, n_pool, PAGE_SIZE, N_HEADS, HEAD_DIM)),
        "cache_v": nrm((DEPTH, n_pool, PAGE_SIZE, N_HEADS, HEAD_DIM)),
        "cache_kidx": nrm((DEPTH, n_pool, PAGE_SIZE, IDX_DIM)),
        "state_conv": nrm((DEPTH, DEC_BATCH, CONV_WIDTH - 1, CONV_CHANNELS)),
        "state_pool": nrm((DEPTH, DEC_BATCH, POOL_STATE, POOL_WIDTH)),
        "page_table": page_table,
        "rel_bias": nrm((NUM_BUCKETS, N_HEADS), 0.5),
        "w_in": nrm((DEPTH, D_MODEL, N_IN), D_MODEL ** -0.5),
        "b_in": nrm((DEPTH, N_IN), 0.02),
        "conv_w": nrm((DEPTH, CONV_WIDTH, CONV_CHANNELS), CONV_WIDTH ** -0.5),
        "pool_w": nrm((DEPTH, len(POOL_WINDOWS), POOL_GROUP, POOL_GROUP), POOL_GROUP ** -0.5),
        "pool_scale": 1.0 + nrm((DEPTH, POOL_WIDTH), 0.02),
        "p_a": nrm((DEPTH, CONV_CHANNELS, D_MODEL), CONV_CHANNELS ** -0.5 * beta),
        "p_b": nrm((DEPTH, POOL_WIDTH, D_MODEL), POOL_WIDTH ** -0.5 * beta),
        "p_c": nrm((DEPTH, ATTN_WIDTH, D_MODEL), ATTN_WIDTH ** -0.5 * beta),
        "w_o": nrm((DEPTH, D_MODEL, D_MODEL), D_MODEL ** -0.5 * beta),
        "ln1_g": 1.0 + nrm((DEPTH, D_MODEL), 0.02),
        "ln1_b": nrm((DEPTH, D_MODEL), 0.02),
        "w_router": nrm((DEPTH, D_MODEL, N_EXPERTS), D_MODEL ** -0.5),
        "b_router": nrm((DEPTH, N_EXPERTS), 0.01),
        "w_gate": nrm((DEPTH, N_EXPERTS, D_MODEL, D_FF), D_MODEL ** -0.5),
        "b_gate": nrm((DEPTH, N_EXPERTS, D_FF), 0.01),
        "w_up": nrm((DEPTH, N_EXPERTS, D_MODEL, D_FF), D_MODEL ** -0.5),
        "b_up": nrm((DEPTH, N_EXPERTS, D_FF), 0.01),
        "w_down": nrm((DEPTH, N_EXPERTS, D_FF, D_MODEL), D_FF ** -0.5 * beta),
        "b_down": nrm((DEPTH, N_EXPERTS, D_MODEL), 0.01),
        "ln2_g": 1.0 + nrm((DEPTH, D_MODEL), 0.02),
        "ln2_b": nrm((DEPTH, D_MODEL), 0.02),
    }


def reference(x_prompt, x_sample, cache_k, cache_v, cache_kidx, state_conv, state_pool, page_table, rel_bias,
              w_in, b_in, conv_w, pool_w, pool_scale, p_a, p_b, p_c, w_o, ln1_g, ln1_b,
              w_router, b_router, w_gate, b_gate, w_up, b_up, w_down, b_down, ln2_g, ln2_b):
    past_len = page_table.shape[1] * PAGE_SIZE
    n_bp = x_prompt.shape[0]
    hp, hs = x_prompt, x_sample
    kp_l, vp_l, kip_l, cp_l, pp_l = [], [], [], [], []
    ks_l, vs_l, kis_l, cs_l, ps_l = [], [], [], [], []
    attn_p = functools.partial(dsa_prompt, rel_bias=rel_bias)
    for l in range(DEPTH):
        lw = (w_in[l], b_in[l], conv_w[l], pool_w[l], pool_scale[l], p_a[l], p_b[l], p_c[l], w_o[l],
              ln1_g[l], ln1_b[l], w_router[l], b_router[l], w_gate[l], b_gate[l], w_up[l], b_up[l],
              w_down[l], b_down[l], ln2_g[l], ln2_b[l])
        conv0 = jnp.zeros((n_bp, CONV_WIDTH - 1, CONV_CHANNELS), hp.dtype)
        pool0 = jnp.zeros((n_bp, POOL_STATE, POOL_WIDTH), hp.dtype)
        hp, kp, vp, kip, cp, pp = layer(hp, 0, conv0, pool0, attn_p, *lw)
        attn_s = functools.partial(dsa_sample, layer=l, cache_k=cache_k, cache_v=cache_v, cache_kidx=cache_kidx,
                                   page_table=page_table, rel_bias=rel_bias)
        hs, ks, vs, kis, cs, ps = layer(hs, past_len, state_conv[l], state_pool[l], attn_s, *lw)
        kp_l.append(kp); vp_l.append(vp); kip_l.append(kip); cp_l.append(cp); pp_l.append(pp)
        ks_l.append(ks); vs_l.append(vs); kis_l.append(kis); cs_l.append(cs); ps_l.append(ps)
    return (hp, hs,
            jnp.stack(kp_l), jnp.stack(vp_l), jnp.stack(kip_l), jnp.stack(cp_l), jnp.stack(pp_l),
            jnp.stack(ks_l), jnp.stack(vs_l), jnp.stack(kis_l), jnp.stack(cs_l), jnp.stack(ps_l))
```

```python
import functools
import math

import numpy as np
import jax
import jax.numpy as jnp
from jax import lax
from jax.experimental import pallas as pl
from jax.experimental.pallas import tpu as pltpu

F32 = jnp.float32
BF16 = jnp.bfloat16
I32 = jnp.int32

N_HEADS = 8
HEAD_DIM = 64
N_IDX_HEADS = 8
IDX_DIM = 64
TOPK_MAX = 256
POOL_WINDOWS = (2, 4, 8, 16)
POOL_STATE = max(POOL_WINDOWS) - 1
CONV_WIDTH = 3
N_EXPERTS = 32
TOP_K = 4
SWIGLU_LIMIT = 7.0
SWIGLU_ALPHA = 1.702
LN_EPS = 1e-5
NUM_BUCKETS = 32
MAX_DISTANCE = 128
PAGE_SIZE = 128

LANES = 128
SUBLANES = 8
VMEM_LIMIT = 56 * 1024 * 1024

ROW_TILE = 256
ATT_TILE = 256
MOE_BLOCK = 256
GATHER_ROWS = 1024
HALO = 16
INT_MIN = np.int32(-2 ** 31)
NEG_INF = float("-inf")


def _cparams(*sem):
    return pltpu.CompilerParams(dimension_semantics=sem, vmem_limit_bytes=VMEM_LIMIT)


def _round_up(n, m):
    return (n + m - 1) // m * m


def _pick_tile(n, candidates):
    for c in candidates:
        if n % c == 0:
            return c
    raise ValueError(f"no tile for {n} in {candidates}")


def _single(block_shape, index_map):
    return pl.BlockSpec(block_shape, index_map, pipeline_mode=pl.Buffered(1))


def _inproj_kernel(x_ref, w_ref, b_ref, o_ref):
    x = x_ref[...].astype(BF16)
    o_ref[...] = jnp.dot(x, w_ref[...], preferred_element_type=F32) + b_ref[...]


def _inproj(x, w, b):
    m, k = x.shape
    n = w.shape[1]
    tm = _pick_tile(m, (1280, 1024, 768, 512, 256))
    tn = _pick_tile(n, (1536, 1280, 1024, 768, 512, 256, 128))
    return pl.pallas_call(
        _inproj_kernel,
        grid=(n // tn, m // tm),
        in_specs=[
            pl.BlockSpec((tm, k), lambda j, i: (i, 0)),
            pl.BlockSpec((k, tn), lambda j, i: (0, j)),
            pl.BlockSpec((1, tn), lambda j, i: (0, j)),
        ],
        out_specs=pl.BlockSpec((tm, tn), lambda j, i: (i, j)),
        out_shape=jax.ShapeDtypeStruct((m, n), F32),
        compiler_params=_cparams("parallel", "parallel"),
        name="inproj",
    )(x, w, b)


def _pool_out(diffs, pw_ref, ps_ref):
    outs = [jnp.dot(d.astype(BF16), pw_ref[g], preferred_element_type=F32) for g, d in enumerate(diffs)]
    return jnp.concatenate(outs, axis=-1) * ps_ref[...]


def _mixer_prompt_kernel(ab_ref, ac_ref, ah_ref, up_ref, cw_ref, pw_ref, ps_ref, ya_ref, yb_ref, eu_ref, ep_ref):
    si = pl.program_id(1)
    ts = ab_ref.shape[0]
    pg = up_ref.shape[1] // len(POOL_WINDOWS)

    @pl.when(si == 0)
    def _():
        eu_ref[0:HALO, :] = jnp.zeros((HALO, eu_ref.shape[1]), F32)
        ep_ref[0:HALO, :] = jnp.zeros((HALO, ep_ref.shape[1]), F32)

    @pl.when(si > 0)
    def _():
        eu_ref[0:HALO, :] = eu_ref[ts:ts + HALO, :]
        ep_ref[0:HALO, :] = ep_ref[ts:ts + HALO, :]

    u = ac_ref[...] * ah_ref[...]
    eu_ref[HALO:HALO + ts, :] = u
    ep_ref[HALO:HALO + ts, :] = up_ref[...]
    conv = cw_ref[0:1, :] * eu_ref[HALO - 2:HALO - 2 + ts, :]
    conv = conv + cw_ref[1:2, :] * eu_ref[HALO - 1:HALO - 1 + ts, :]
    conv = conv + cw_ref[2:3, :] * u
    ya_ref[...] = (ab_ref[...] * conv).astype(ya_ref.dtype)

    pos = si * ts + lax.broadcasted_iota(I32, (ts, 1), 0)
    diffs = []
    for g, w in enumerate(POOL_WINDOWS):
        sl = slice(g * pg, (g + 1) * pg)
        win = ep_ref[HALO:HALO + ts, sl]
        for j in range(1, w):
            win = win + ep_ref[HALO - j:HALO - j + ts, sl]
        cnt = jnp.minimum(w, pos + 1).astype(F32)
        diffs.append(win / cnt - up_ref[:, sl])
    yb_ref[...] = _pool_out(diffs, pw_ref, ps_ref).astype(yb_ref.dtype)


def _mixer_prompt(z, col0, n_b, seq, cc, conv_w, pool_w_bf, pool_scale):
    ts = _pick_tile(seq, (512, 256, 128))
    nblk = seq // ts
    cb = col0 // cc

    def zspec(i):
        return pl.BlockSpec((ts, cc), lambda b, s, i=i: (b * nblk + s, cb + i))

    out_spec = pl.BlockSpec((ts, cc), lambda b, s: (b * nblk + s, 0))
    return pl.pallas_call(
        _mixer_prompt_kernel,
        grid=(n_b, nblk),
        in_specs=[zspec(0), zspec(1), zspec(2), zspec(3),
                  pl.BlockSpec(conv_w.shape, lambda b, s: (0, 0)),
                  pl.BlockSpec(pool_w_bf.shape, lambda b, s: (0, 0, 0)),
                  pl.BlockSpec(pool_scale.shape, lambda b, s: (0, 0))],
        out_specs=[out_spec, out_spec],
        out_shape=[jax.ShapeDtypeStruct((n_b * seq, cc), BF16)] * 2,
        scratch_shapes=[pltpu.VMEM((HALO + ts, cc), F32), pltpu.VMEM((HALO + ts, cc), F32)],
        compiler_params=_cparams("arbitrary", "arbitrary"),
        name="mixer_prompt",
    )(z, z, z, z, conv_w, pool_w_bf, pool_scale)


def _mixer_sample_kernel(pos0, ab_ref, ac_ref, ah_ref, up_ref, cprev_ref, pprev_ref, cw_ref, pw_ref, ps_ref,
                         ya_ref, yb_ref, u_ref):
    pg = up_ref.shape[1] // len(POOL_WINDOWS)
    u = ac_ref[...] * ah_ref[...]
    u_ref[...] = u
    conv = cw_ref[0:1, :] * cprev_ref[0] + cw_ref[1:2, :] * cprev_ref[1] + cw_ref[2:3, :] * u
    ya_ref[...] = (ab_ref[...] * conv).astype(ya_ref.dtype)
    diffs = []
    for g, w in enumerate(POOL_WINDOWS):
        sl = slice(g * pg, (g + 1) * pg)
        win = up_ref[:, sl]
        for j in range(1, w):
            win = win + pprev_ref[POOL_STATE - j, :, sl]
        cnt = float(min(w, pos0 + 1))
        diffs.append(win / cnt - up_ref[:, sl])
    yb_ref[...] = _pool_out(diffs, pw_ref, ps_ref).astype(yb_ref.dtype)


def _mixer_sample(a_b, a_c, a_h, u_pool, conv_prev_t, pool_prev_t, pos0, conv_w, pool_w_bf, pool_scale):
    n, cc = a_b.shape
    return pl.pallas_call(
        functools.partial(_mixer_sample_kernel, pos0),
        out_shape=[jax.ShapeDtypeStruct((n, cc), BF16), jax.ShapeDtypeStruct((n, cc), BF16),
                   jax.ShapeDtypeStruct((n, cc), F32)],
        compiler_params=pltpu.CompilerParams(vmem_limit_bytes=VMEM_LIMIT),
        name="mixer_sample",
    )(a_b, a_c, a_h, u_pool, conv_prev_t, pool_prev_t, conv_w, pool_w_bf, pool_scale)


def _sortable_key(score):
    bits = lax.bitcast_convert_type(score, I32)
    return bits ^ ((bits >> 31) & np.int32(0x7FFFFFFF))


def _kth_largest_key(count_ge, k, shape):
    def bit_body(it, thr_u):
        bit = jnp.left_shift(jnp.int32(1), 31 - it)
        cand_u = thr_u | bit
        cnt = count_ge(cand_u ^ INT_MIN)
        return jnp.where(cnt >= k, cand_u, thr_u)

    thr_u = lax.fori_loop(0, 32, bit_body, jnp.zeros(shape, I32))
    return thr_u ^ INT_MIN


def _dsa_prompt_kernel(n_sel, qi_t_ref, w_t_ref, kib_ref, q_t_ref, k_ref, v_t_ref, bias_ref, o_ref,
                       key_scr, mask_scr, qz_scr, acc_scr, m_scr, l_scr):
    qi = pl.program_id(1)
    tq = o_ref.shape[0]
    tk = tq
    n_heads = N_HEADS
    dh = HEAD_DIM

    def score_chunk(c):
        r0 = pl.multiple_of(c * tk, tk)
        kc = kib_ref[pl.ds(r0, tk), :]
        acc = jnp.zeros((tk, tq), F32)
        for h in range(N_IDX_HEADS):
            d = jnp.dot(kc, qi_t_ref[h * IDX_DIM:(h + 1) * IDX_DIM, :], preferred_element_type=F32)
            acc = acc + w_t_ref[h:h + 1, :] * jnp.maximum(d, 0.0)
        return acc

    def score_body(c, carry):
        key_scr[c] = _sortable_key(score_chunk(c))
        return carry

    lax.fori_loop(0, qi, score_body, 0)
    key_pos = lax.broadcasted_iota(I32, (tk, tq), 0)
    qry_pos = lax.broadcasted_iota(I32, (tk, tq), 1)
    diag = jnp.where(key_pos <= qry_pos, score_chunk(qi), NEG_INF)
    key_scr[qi] = _sortable_key(diag)

    def count_where(pred):
        def body(c, acc):
            ind = jnp.where(pred(key_scr[c], c), 1, 0).astype(I32)
            return acc + ind.reshape(tk // SUBLANES, SUBLANES, tq).sum(axis=0)

        acc = lax.fori_loop(0, qi + 1, body, jnp.zeros((SUBLANES, tq), I32))
        return acc.sum(axis=0, keepdims=True)

    thr = _kth_largest_key(lambda cand: count_where(lambda blk, c: blk >= cand), n_sel, (1, tq))

    cnt_gt = count_where(lambda blk, c: blk > thr)
    cnt_ge = count_where(lambda blk, c: blk >= thr)
    need = n_sel - cnt_gt
    neg_inf_key = _sortable_key(jnp.full((1, tq), NEG_INF, F32))
    overflow = (cnt_ge > n_sel) & (thr > neg_inf_key)

    @pl.when(jnp.max(overflow.astype(I32)) > 0)
    def _():
        def count_tied_below(bound):
            return count_where(lambda blk, c: (blk == thr) & (c * tk + key_pos < bound))

        def bis_body(it, lohi):
            lo, hi = lohi
            mid = (lo + hi) // 2
            ok = count_tied_below(mid) >= need
            return jnp.where(ok, lo, mid), jnp.where(ok, mid, hi)

        n_keys = (qi + 1) * tk
        lo0 = jnp.zeros((1, tq), I32)
        hi0 = jnp.zeros((1, tq), I32) + n_keys
        n_it = int(math.ceil(math.log2(key_scr.shape[0] * tk))) + 1
        _, bound = lax.fori_loop(0, n_it, bis_body, (lo0, hi0))

        def demote(c, carry):
            blk = key_scr[c]
            drop = overflow & (blk == thr) & (c * tk + key_pos >= bound)
            key_scr[c] = jnp.where(drop, thr - 1, blk)
            return carry

        lax.fori_loop(0, qi + 1, demote, 0)

    zero_half = jnp.zeros((dh, tq), BF16)
    for h in range(n_heads):
        qh = q_t_ref[h * dh:(h + 1) * dh, :]
        pair = (qh, zero_half) if h % 2 == 0 else (zero_half, qh)
        qz_scr[h] = jnp.concatenate(pair, axis=0)
    m_scr[...] = jnp.full(m_scr.shape, -1e30, F32)
    l_scr[...] = jnp.zeros(l_scr.shape, F32)
    acc_scr[...] = jnp.zeros(acc_scr.shape, F32)

    def attend(c, bias_idx):
        r0 = pl.multiple_of(c * tk, tk)
        mask_scr[...] = jnp.where(key_scr[c] >= thr, 0.0, NEG_INF).astype(F32)
        for h in range(n_heads):
            p2 = h // 2
            kc = k_ref[pl.ds(r0, tk), p2 * 2 * dh:(p2 + 1) * 2 * dh]
            logit = jnp.dot(kc, qz_scr[h], preferred_element_type=F32) + mask_scr[...]
            if bias_idx is not None:
                logit = logit + bias_ref[bias_idx, h]
            m_old = m_scr[h:h + 1, :]
            m_new = jnp.maximum(m_old, jnp.max(logit, axis=0, keepdims=True))
            alpha = jnp.exp(m_old - m_new)
            p = jnp.exp(logit - m_new)
            l_scr[h:h + 1, :] = alpha * l_scr[h:h + 1, :] + jnp.sum(p, axis=0, keepdims=True)
            m_scr[h:h + 1, :] = m_new
            pv = jnp.dot(v_t_ref[h * dh:(h + 1) * dh, pl.ds(r0, tk)], p.astype(BF16), preferred_element_type=F32)
            acc_scr[h * dh:(h + 1) * dh, :] = alpha * acc_scr[h * dh:(h + 1) * dh, :] + pv

    def far_body(c, carry):
        attend(c, None)
        return carry

    lax.fori_loop(0, jnp.maximum(qi - 1, 0), far_body, 0)

    @pl.when(qi >= 1)
    def _():
        attend(qi - 1, 1)

    attend(qi, 0)

    inv = 1.0 / l_scr[...]
    rows = [acc_scr[h * dh:(h + 1) * dh, :] * inv[h:h + 1, :] for h in range(n_heads)]
    o_ref[...] = jnp.concatenate(rows, axis=0).T.astype(o_ref.dtype)


def _dsa_prompt(qi_t, w_t, kib, q_t, k_bf, v_t, bias_tab, n_sel):
    n_b, width, seq = q_t.shape
    tq = ATT_TILE
    nq = seq // tq
    kernel = functools.partial(_dsa_prompt_kernel, n_sel)
    return pl.pallas_call(
        kernel,
        grid=(n_b, nq),
        in_specs=[
            pl.BlockSpec((None, qi_t.shape[1], tq), lambda b, q: (b, 0, q)),
            pl.BlockSpec((None, w_t.shape[1], tq), lambda b, q: (b, 0, q)),
            _single((None, seq, kib.shape[2]), lambda b, q: (b, 0, 0)),
            pl.BlockSpec((None, width, tq), lambda b, q: (b, 0, q)),
            _single((None, seq, width), lambda b, q: (b, 0, 0)),
            _single((None, width, seq), lambda b, q: (b, 0, 0)),
            _single(bias_tab.shape, lambda b, q: (0, 0, 0, 0)),
        ],
        out_specs=pl.BlockSpec((None, tq, width), lambda b, q: (b, q, 0)),
        out_shape=jax.ShapeDtypeStruct((n_b, seq, width), BF16),
        scratch_shapes=[
            pltpu.VMEM((nq, tq, tq), I32),
            pltpu.VMEM((tq, tq), F32),
            pltpu.VMEM((N_HEADS, 2 * HEAD_DIM, tq), BF16),
            pltpu.VMEM((width, tq), F32),
            pltpu.VMEM((N_HEADS, tq), F32),
            pltpu.VMEM((N_HEADS, tq), F32),
        ],
        compiler_params=_cparams("arbitrary", "arbitrary"),
        name="dsa_prompt",
    )(qi_t, w_t, kib, q_t, k_bf, v_t, bias_tab)


def _sample_score_kernel(layer, n_pages, pt_ref, qi_ref, w_ref, knew_ref, kcache_ref, o_ref, kbuf, sem):
    b = pl.program_id(0)

    def page_copy(p):
        return pltpu.make_async_copy(kcache_ref.at[layer, pt_ref[b, p]], kbuf.at[p], sem)

    def start(p, c):
        page_copy(p).start()
        return c

    lax.fori_loop(0, n_pages, start, 0)

    def wait(p, c):
        page_copy(p).wait()
        return c

    lax.fori_loop(0, n_pages, wait, 0)

    qi = qi_ref[0]
    w = w_ref[0]

    def score_rows(keys):
        d = lax.dot_general(qi, keys.astype(BF16), (((1,), (1,)), ((), ())), preferred_element_type=F32)
        return jnp.sum(w * jnp.maximum(d, 0.0), axis=0, keepdims=True)

    def page_body(p, c):
        o_ref[0, pl.ds(p, 1), :] = score_rows(kbuf[p])
        return c

    lax.fori_loop(0, n_pages, page_body, 0)
    own = score_rows(knew_ref[0])
    lane = lax.broadcasted_iota(I32, own.shape, 1)
    o_ref[0, n_pages:n_pages + 1, :] = jnp.where(lane == 0, own, NEG_INF)
    n_rows = o_ref.shape[1]
    if n_rows > n_pages + 1:
        o_ref[0, n_pages + 1:, :] = jnp.full((n_rows - n_pages - 1, PAGE_SIZE), NEG_INF, F32)


def _sample_scores(layer, page_table, qi_s, w_s, kidx_new_pad, cache_kidx):
    n_b, n_pages = page_table.shape
    n_rows = _round_up(n_pages + 1, LANES)
    grid_spec = pltpu.PrefetchScalarGridSpec(
        num_scalar_prefetch=1,
        grid=(n_b,),
        in_specs=[
            pl.BlockSpec((1, N_IDX_HEADS, IDX_DIM), lambda b, pt: (b, 0, 0)),
            pl.BlockSpec((1, N_IDX_HEADS, 1), lambda b, pt: (b, 0, 0)),
            pl.BlockSpec((1, PAGE_SIZE, IDX_DIM), lambda b, pt: (b, 0, 0)),
            pl.BlockSpec(memory_space=pl.ANY),
        ],
        out_specs=pl.BlockSpec((1, n_rows, PAGE_SIZE), lambda b, pt: (b, 0, 0)),
        scratch_shapes=[pltpu.VMEM((n_pages, PAGE_SIZE, IDX_DIM), F32), pltpu.SemaphoreType.DMA],
    )
    return pl.pallas_call(
        functools.partial(_sample_score_kernel, layer, n_pages),
        grid_spec=grid_spec,
        out_shape=jax.ShapeDtypeStruct((n_b, n_rows, PAGE_SIZE), F32),
        compiler_params=_cparams("arbitrary"),
        name="sample_scores",
    )(page_table, qi_s, w_s, kidx_new_pad, cache_kidx)


def _sample_select_kernel(n_sel, n_live, s_ref, o_ref, rank_scr):
    score = s_ref[0]
    n_rows = score.shape[0]
    key = _sortable_key(score)
    pos = lax.broadcasted_iota(I32, key.shape, 0) * PAGE_SIZE + lax.broadcasted_iota(I32, key.shape, 1)

    def count(pred):
        return jnp.sum(jnp.where(pred, 1, 0).astype(I32), keepdims=True)

    thr = _kth_largest_key(lambda cand: count(key >= cand), n_sel, (1, 1))
    need = n_sel - count(key > thr)

    def bis_body(it, lohi):
        lo, hi = lohi
        mid = (lo + hi) // 2
        ok = count((key == thr) & (pos < mid)) >= need
        return jnp.where(ok, lo, mid), jnp.where(ok, mid, hi)

    n_it = int(math.ceil(math.log2(n_rows * PAGE_SIZE))) + 1
    _, bound = lax.fori_loop(0, n_it, bis_body,
                             (jnp.zeros((1, 1), I32), jnp.full((1, 1), n_rows * PAGE_SIZE, I32)))
    sel = (key > thr) | ((key == thr) & (pos < bound))
    m = jnp.where(sel, 1.0, 0.0).astype(BF16)

    li = lax.broadcasted_iota(I32, (PAGE_SIZE, PAGE_SIZE), 0)
    lj = lax.broadcasted_iota(I32, (PAGE_SIZE, PAGE_SIZE), 1)
    upper = jnp.where(li <= lj, 1.0, 0.0).astype(BF16)
    incl = jnp.dot(m, upper, preferred_element_type=F32)
    ri = lax.broadcasted_iota(I32, (n_rows, n_rows), 0)
    rj = lax.broadcasted_iota(I32, (n_rows, n_rows), 1)
    lower = jnp.where(rj < ri, 1.0, 0.0).astype(BF16)
    before = jnp.sum(jnp.dot(lower, m, preferred_element_type=F32), axis=1, keepdims=True)
    rank = incl - m.astype(F32) + before
    rank_scr[...] = jnp.where(sel, rank, -1.0).astype(I32)

    slot = lax.broadcasted_iota(I32, (n_sel, PAGE_SIZE), 0)
    lane = lax.broadcasted_iota(I32, (n_sel, PAGE_SIZE), 1)

    def row_body(r, acc):
        hit = rank_scr[pl.ds(r, 1), :] == slot
        return acc + jnp.where(hit, r * PAGE_SIZE + lane, 0)

    acc = lax.fori_loop(0, n_live, row_body, jnp.zeros((n_sel, PAGE_SIZE), I32))
    o_ref[0] = jnp.sum(acc, axis=1, keepdims=True)


def _sample_select(scores, n_sel, n_live):
    n_b, n_rows, _ = scores.shape
    return pl.pallas_call(
        functools.partial(_sample_select_kernel, n_sel, n_live),
        grid=(n_b,),
        in_specs=[pl.BlockSpec((1, n_rows, PAGE_SIZE), lambda b: (b, 0, 0))],
        out_specs=pl.BlockSpec((1, n_sel, 1), lambda b: (b, 0, 0)),
        out_shape=jax.ShapeDtypeStruct((n_b, n_sel, 1), I32),
        scratch_shapes=[pltpu.VMEM((n_rows, PAGE_SIZE), I32)],
        compiler_params=_cparams("arbitrary"),
        name="sample_select",
    )(scores)


def _sample_attend_kernel(layer, n_pages, n_sel, pt_ref, sel_ref, q_ref, bias_ref, knew_ref, vnew_ref,
                          kcache_ref, vcache_ref, o_ref, kbuf, vbuf, sem):
    b = pl.program_id(0)
    past = n_pages * PAGE_SIZE

    def copies(j):
        s = sel_ref[b, j]
        sp = jnp.minimum(s, past - 1)
        page = pt_ref[b, sp // PAGE_SIZE]
        row = sp % PAGE_SIZE
        return (s,
                pltpu.make_async_copy(kcache_ref.at[layer, page, row], kbuf.at[j], sem.at[0]),
                pltpu.make_async_copy(vcache_ref.at[layer, page, row], vbuf.at[j], sem.at[1]),
                pltpu.make_async_copy(knew_ref.at[b], kbuf.at[j], sem.at[0]),
                pltpu.make_async_copy(vnew_ref.at[b], vbuf.at[j], sem.at[1]))

    def start(j, c):
        s, ck, cv, nk, nv = copies(j)

        @pl.when(s < past)
        def _():
            ck.start()
            cv.start()

        @pl.when(s >= past)
        def _():
            nk.start()
            nv.start()

        return c

    lax.fori_loop(0, n_sel, start, 0)

    def wait(j, c):
        _, ck, cv, _, _ = copies(j)
        ck.wait()
        cv.wait()
        return c

    lax.fori_loop(0, n_sel, wait, 0)

    q = q_ref[0]
    k = kbuf[...]
    logit = jnp.sum(k * q[None], axis=-1, keepdims=True) + bias_ref[0]
    m = jnp.max(logit, axis=0, keepdims=True)
    p = jnp.exp(logit - m)
    p = p / jnp.sum(p, axis=0, keepdims=True)
    o_ref[0] = jnp.sum(p * vbuf[...], axis=0)


def _sample_attend(layer, page_table, sel, q_s, bias_sel, k_new, v_new, cache_k, cache_v):
    n_b, n_pages = page_table.shape
    n_sel = sel.shape[1]
    grid_spec = pltpu.PrefetchScalarGridSpec(
        num_scalar_prefetch=2,
        grid=(n_b,),
        in_specs=[
            pl.BlockSpec((1, N_HEADS, HEAD_DIM), lambda b, pt, sl: (b, 0, 0)),
            pl.BlockSpec((1, n_sel, N_HEADS, 1), lambda b, pt, sl: (b, 0, 0, 0)),
            pl.BlockSpec(memory_space=pl.ANY),
            pl.BlockSpec(memory_space=pl.ANY),
            pl.BlockSpec(memory_space=pl.ANY),
            pl.BlockSpec(memory_space=pl.ANY),
        ],
        out_specs=pl.BlockSpec((1, N_HEADS, HEAD_DIM), lambda b, pt, sl: (b, 0, 0)),
        scratch_shapes=[pltpu.VMEM((n_sel, N_HEADS, HEAD_DIM), F32), pltpu.VMEM((n_sel, N_HEADS, HEAD_DIM), F32),
                        pltpu.SemaphoreType.DMA((2,))],
    )
    return pl.pallas_call(
        functools.partial(_sample_attend_kernel, layer, n_pages, n_sel),
        grid_spec=grid_spec,
        out_shape=jax.ShapeDtypeStruct((n_b, N_HEADS, HEAD_DIM), F32),
        compiler_params=_cparams("arbitrary"),
        name="sample_attend",
    )(page_table, sel, q_s, bias_sel, k_new, v_new, cache_k, cache_v)


def _layer_norm(x, g, b):
    mu = jnp.mean(x, axis=-1, keepdims=True)
    xc = x - mu
    var = jnp.mean(xc * xc, axis=-1, keepdims=True)
    return xc * lax.rsqrt(var + LN_EPS) * g + b


def _split_bf16(x):
    hi = x.astype(BF16)
    lo = (x - hi.astype(F32)).astype(BF16)
    return hi, lo


def _merge_kernel(alpha, x_ref, ga_ref, gb_ref, gc_ref, ya_ref, yb_ref, yc_ref, pa_ref, pb_ref, pc_ref, wo_ref,
                  g1_ref, b1_ref, wr_hi_ref, wr_lo_ref, br_ref, x1_ref, te_ref, tg_ref):
    def branch(g_ref, y_ref, p_ref):
        return jax.nn.sigmoid(g_ref[...]) * jnp.dot(y_ref[...], p_ref[...], preferred_element_type=F32)

    merged = branch(ga_ref, ya_ref, pa_ref) + branch(gb_ref, yb_ref, pb_ref) + branch(gc_ref, yc_ref, pc_ref)
    y = alpha * x_ref[...] + jnp.dot(merged.astype(BF16), wo_ref[...], preferred_element_type=F32)
    x1 = _layer_norm(y, g1_ref[...], b1_ref[...])
    x1_ref[...] = x1

    x_hi, x_lo = _split_bf16(x1)
    dims = (((1,), (1,)), ((), ()))
    logits = lax.dot_general(wr_hi_ref[...], x_hi, dims, preferred_element_type=F32)
    logits = logits + lax.dot_general(wr_hi_ref[...], x_lo, dims, preferred_element_type=F32)
    logits = logits + lax.dot_general(wr_lo_ref[...], x_hi, dims, preferred_element_type=F32)
    logits = logits + br_ref[...]
    eidx = lax.broadcasted_iota(I32, logits.shape, 0)
    vals, idxs = [], []
    cur = logits
    for _ in range(TOP_K):
        v = jnp.max(cur, axis=0, keepdims=True)
        i = jnp.min(jnp.where(cur == v, eidx, N_EXPERTS), axis=0, keepdims=True)
        vals.append(v)
        idxs.append(i)
        cur = jnp.where(eidx == i, NEG_INF, cur)
    top_v = jnp.concatenate(vals, axis=0)
    e = jnp.exp(top_v - vals[0])
    tg_ref[...] = e / jnp.sum(e, axis=0, keepdims=True)
    te_ref[...] = jnp.concatenate(idxs, axis=0)


def _merge(alpha, x, z, y_a, y_b, y_c, p_a, p_b, p_c, w_o, ln_g, ln_b, wr_hi, wr_lo, b_r):
    t, d = x.shape
    cc = y_a.shape[1]
    tm = ROW_TILE

    def const(a):
        return pl.BlockSpec(a.shape, lambda i: (0,) * a.ndim)

    def rows(width, col=0):
        return pl.BlockSpec((tm, width), lambda i, col=col: (i, col))

    return pl.pallas_call(
        functools.partial(_merge_kernel, alpha),
        grid=(t // tm,),
        in_specs=[rows(d), rows(d, 0), rows(d, 1), rows(d, 2), rows(cc), rows(cc), rows(cc),
                  const(p_a), const(p_b), const(p_c), const(w_o), const(ln_g), const(ln_b),
                  const(wr_hi), const(wr_lo), const(b_r)],
        out_specs=[rows(d), pl.BlockSpec((TOP_K, tm), lambda i: (0, i)), pl.BlockSpec((TOP_K, tm), lambda i: (0, i))],
        out_shape=[jax.ShapeDtypeStruct((t, d), F32), jax.ShapeDtypeStruct((TOP_K, t), I32),
                   jax.ShapeDtypeStruct((TOP_K, t), F32)],
        compiler_params=_cparams("parallel"),
        name="merge_ln_router",
    )(x, z, z, z, y_a, y_b, y_c, p_a, p_b, p_c, w_o, ln_g, ln_b, wr_hi, wr_lo, b_r)


def _gather_kernel(idx_ref, src_ref, o_ref, idx_smem, isem, dsem):
    i = pl.program_id(0)
    n = pl.num_programs(0)
    slot = i % 2
    g = GATHER_ROWS

    def idx_copy(step, s):
        return pltpu.make_async_copy(idx_ref.at[pl.ds(step * SUBLANES, SUBLANES), :], idx_smem.at[s], isem.at[s])

    @pl.when(i == 0)
    def _():
        idx_copy(0, 0).start()

    @pl.when(i + 1 < n)
    def _():
        idx_copy(i + 1, 1 - slot).start()

    idx_copy(i, slot).wait()

    def row_copy(r, s):
        t = idx_smem[s, r // LANES, r % LANES]
        return pltpu.make_async_copy(src_ref.at[pl.ds(t, 1), :], o_ref.at[pl.ds(i * g + r, 1), :], dsem)

    def start(r, c):
        row_copy(r, slot).start()
        return c

    lax.fori_loop(0, g, start, 0)

    def wait(r, c):
        row_copy(r, slot).wait()
        return c

    lax.fori_loop(0, g, wait, 0)


def _row_gather(src, idx):
    n_out = idx.shape[0]
    assert n_out % GATHER_ROWS == 0
    idx2 = idx.reshape(n_out // LANES, LANES)
    return pl.pallas_call(
        _gather_kernel,
        grid=(n_out // GATHER_ROWS,),
        in_specs=[pl.BlockSpec(memory_space=pl.ANY), pl.BlockSpec(memory_space=pl.ANY)],
        out_specs=pl.BlockSpec(memory_space=pl.ANY),
        out_shape=jax.ShapeDtypeStruct((n_out, src.shape[1]), src.dtype),
        scratch_shapes=[pltpu.SMEM((2, SUBLANES, LANES), I32), pltpu.SemaphoreType.DMA((2,)),
                        pltpu.SemaphoreType.DMA],
        compiler_params=_cparams("arbitrary"),
        name="row_gather",
    )(idx2, src)


def _moe_kernel(blk_e_ref, n_used_ref, x_ref, wg_ref, bg_ref, wu_ref, bu_ref, wd_ref, bd_ref, o_ref):
    i = pl.program_id(0)

    @pl.when(i < n_used_ref[0])
    def _():
        xe = x_ref[...].astype(BF16)
        g = jnp.minimum(jnp.dot(xe, wg_ref[0], preferred_element_type=F32) + bg_ref[0], SWIGLU_LIMIT)
        up = jnp.clip(jnp.dot(xe, wu_ref[0], preferred_element_type=F32) + bu_ref[0], -SWIGLU_LIMIT, SWIGLU_LIMIT)
        h = g * jax.nn.sigmoid(SWIGLU_ALPHA * g) * (up + 1.0)
        o_ref[...] = jnp.dot(h.astype(BF16), wd_ref[0], preferred_element_type=F32) + bd_ref[0]

    @pl.when(i >= n_used_ref[0])
    def _():
        o_ref[...] = jnp.zeros(o_ref.shape, o_ref.dtype)


def _moe_experts(blk_e, n_used, xs, w_gate, b_gate, w_up, b_up, w_down, b_down):
    n_blocks = blk_e.shape[0]
    bm = MOE_BLOCK
    d = xs.shape[1]
    ff = w_gate.shape[2]

    def wspec(shape):
        return pl.BlockSpec((1,) + shape, lambda i, be, nu: (be[i], 0, 0))

    grid_spec = pltpu.PrefetchScalarGridSpec(
        num_scalar_prefetch=2,
        grid=(n_blocks,),
        in_specs=[pl.BlockSpec((bm, d), lambda i, be, nu: (i, 0)),
                  wspec((d, ff)), wspec((1, ff)), wspec((d, ff)), wspec((1, ff)), wspec((ff, d)), wspec((1, d))],
        out_specs=pl.BlockSpec((bm, d), lambda i, be, nu: (i, 0)),
    )
    return pl.pallas_call(
        _moe_kernel,
        grid_spec=grid_spec,
        out_shape=jax.ShapeDtypeStruct((n_blocks * bm, d), F32),
        compiler_params=_cparams("arbitrary"),
        name="moe_experts",
    )(blk_e, n_used, xs, w_gate, b_gate, w_up, b_up, w_down, b_down)


def _combine_kernel(alpha, x_ref, y_ref, g_ref, lg_ref, lb_ref, o_ref):
    h = g_ref[:, 0:1] * y_ref[0]
    for k in range(1, TOP_K):
        h = h + g_ref[:, k:k + 1] * y_ref[k]
    o_ref[...] = _layer_norm(alpha * x_ref[...] + h, lg_ref[...], lb_ref[...])


def _combine(alpha, x1, y_kt, gate_tk, ln_g, ln_b):
    t, d = x1.shape
    tm = ROW_TILE
    return pl.pallas_call(
        functools.partial(_combine_kernel, alpha),
        grid=(t // tm,),
        in_specs=[pl.BlockSpec((tm, d), lambda i: (i, 0)),
                  pl.BlockSpec((TOP_K, tm, d), lambda i: (0, i, 0)),
                  pl.BlockSpec((tm, TOP_K), lambda i: (i, 0)),
                  pl.BlockSpec(ln_g.shape, lambda i: (0, 0)),
                  pl.BlockSpec(ln_b.shape, lambda i: (0, 0))],
        out_specs=pl.BlockSpec((tm, d), lambda i: (i, 0)),
        out_shape=jax.ShapeDtypeStruct((t, d), F32),
        compiler_params=_cparams("parallel"),
        name="combine_ln",
    )(x1, y_kt, gate_tk, ln_g, ln_b)


def _moe(alpha, x1, top_e, top_g, w_gate, b_gate, w_up, b_up, w_down, b_down, ln_g, ln_b):
    t, d = x1.shape
    bm = MOE_BLOCK
    n_assign = t * TOP_K
    flat_e = top_e.T.reshape(-1)
    order = jnp.argsort(flat_e).astype(I32)
    se = flat_e[order]
    counts = jnp.bincount(flat_e, length=N_EXPERTS).astype(I32)
    padded = (counts + bm - 1) // bm * bm
    pad_end = jnp.cumsum(padded)
    pad_start = pad_end - padded
    grp_start = jnp.cumsum(counts) - counts
    dest_sorted = pad_start[se] + jnp.arange(n_assign, dtype=I32) - grp_start[se]
    n_blocks = -(-n_assign // bm) + N_EXPERTS
    n_rows = n_blocks * bm
    n_rows_g = _round_up(n_rows, GATHER_ROWS)
    row_token = jnp.zeros((n_rows_g,), I32).at[dest_sorted].set(order // TOP_K)
    pos = jnp.zeros((n_assign,), I32).at[order].set(dest_sorted)
    blk_e = jnp.minimum(jnp.searchsorted(pad_end, jnp.arange(n_blocks, dtype=I32) * bm, side="right"),
                        N_EXPERTS - 1).astype(I32)
    n_used = (pad_end[-1:] // bm).astype(I32)

    xs = _row_gather(x1, row_token)
    ys = _moe_experts(blk_e, n_used, xs, w_gate, b_gate, w_up, b_up, w_down, b_down)
    pos_kt = pos.reshape(t, TOP_K).T.reshape(-1)
    y_kt = _row_gather(ys, pos_kt).reshape(TOP_K, t, d)
    return _combine(alpha, x1, y_kt, top_g.T, ln_g, ln_b)


def _t5_bucket(dist):
    exact = NUM_BUCKETS // 2
    d = jnp.maximum(dist, 1).astype(F32)
    large = exact + (jnp.log(d / exact) / math.log(MAX_DISTANCE / exact) * (NUM_BUCKETS - exact)).astype(I32)
    return jnp.where(dist < exact, dist, jnp.minimum(large, NUM_BUCKETS - 1))


def _prompt_bias_tables(rel_bias, tile):
    far = rel_bias[NUM_BUCKETS - 1]
    by_dist = rel_bias[_t5_bucket(jnp.arange(2 * tile, dtype=I32))] - far[None, :]
    j = jnp.arange(tile, dtype=I32)[:, None]
    i = jnp.arange(tile, dtype=I32)[None, :]
    d0 = i - j
    t0 = jnp.where((d0 >= 0)[..., None], by_dist[jnp.maximum(d0, 0)], NEG_INF)
    t1 = by_dist[tile + i - j]
    return jnp.stack([t0, t1]).transpose(0, 3, 1, 2).astype(F32)


def kernel(x_prompt, x_sample, cache_k, cache_v, cache_kidx, state_conv, state_pool, page_table, rel_bias,
           w_in, b_in, conv_w, pool_w, pool_scale, p_a, p_b, p_c, w_o, ln1_g, ln1_b,
           w_router, b_router, w_gate, b_gate, w_up, b_up, w_down, b_down, ln2_g, ln2_b):
    n_b, seq, d = x_prompt.shape
    n_s, dec_seq, _ = x_sample.shape
    assert dec_seq == 1
    depth = w_in.shape[0]
    n_pages = page_table.shape[1]
    past = n_pages * PAGE_SIZE
    cc = conv_w.shape[2]
    pw = pool_scale.shape[1]
    aw = N_HEADS * HEAD_DIM
    iw = N_IDX_HEADS * IDX_DIM
    assert cc == pw == aw == iw and seq % ATT_TILE == 0 and d % cc == 0
    alpha = float((2 * depth) ** 0.25)
    n_sel_p = min(TOPK_MAX, seq // 4)
    n_sel_s = min(TOPK_MAX, (past + dec_seq) // 4)

    t_p = n_b * seq
    t_all = _round_up(t_p + n_s, ROW_TILE)
    x = jnp.concatenate([x_prompt.reshape(t_p, d), x_sample.reshape(n_s, d),
                         jnp.zeros((t_all - t_p - n_s, d), F32)], axis=0)

    sizes = (cc, cc, cc, pw, aw, aw, aw, iw, IDX_DIM, N_IDX_HEADS, 3 * d)
    n_in = sum(sizes)
    off_g = n_in - 3 * d
    nz = _round_up(n_in, 1536)
    c_ab = 3 * d
    c_q = c_ab + 4 * cc
    c_ki = c_q + 4 * cc
    bias_tab = _prompt_bias_tables(rel_bias, ATT_TILE)
    bias_far = rel_bias[NUM_BUCKETS - 1]
    pt = page_table.astype(I32)

    outs = {name: [] for name in ("kp", "vp", "kip", "cp", "pp", "ks", "vs", "kis", "cs", "ps")}
    for l in range(depth):
        w_l = jnp.concatenate([w_in[l][:, off_g:], w_in[l][:, :off_g], jnp.zeros((d, nz - n_in), F32)], axis=1)
        b_l = jnp.concatenate([b_in[l][off_g:], b_in[l][:off_g], jnp.zeros((nz - n_in,), F32)])[None, :]
        z = _inproj(x, w_l.astype(BF16), b_l)

        def col(c0, width, rows=slice(None)):
            return z[rows, c0:c0 + width]

        pr = slice(0, t_p)
        sr = slice(t_p, t_p + n_s)
        conv_w_l = conv_w[l]
        pool_w_l = pool_w[l].astype(BF16)
        pool_s_l = pool_scale[l][None, :]

        ya_p, yb_p = _mixer_prompt(z, c_ab, n_b, seq, cc, conv_w_l, pool_w_l, pool_s_l)

        q_p = col(c_q, aw, pr).reshape(n_b, seq, aw)
        k_p = col(c_q + aw, aw, pr).reshape(n_b, seq, aw)
        v_p = col(c_q + 2 * aw, aw, pr).reshape(n_b, seq, aw)
        qi_p = col(c_q + 3 * aw, iw, pr).reshape(n_b, seq, iw)
        ki_p = col(c_ki, IDX_DIM, pr).reshape(n_b, seq, IDX_DIM)
        wi_p = col(c_ki + IDX_DIM, N_IDX_HEADS, pr).reshape(n_b, seq, N_IDX_HEADS)
        qi_t = (qi_p * (IDX_DIM ** -0.5)).astype(BF16).transpose(0, 2, 1)
        w_t = (wi_p * (N_IDX_HEADS ** -0.5)).transpose(0, 2, 1)
        q_t = (q_p * (HEAD_DIM ** -0.5)).astype(BF16).transpose(0, 2, 1)
        yc_p = _dsa_prompt(qi_t, w_t, ki_p.astype(BF16), q_t, k_p.astype(BF16),
                           v_p.astype(BF16).transpose(0, 2, 1), bias_tab, n_sel_p)

        ya_s, yb_s, u_s = _mixer_sample(
            col(c_ab, cc, sr), col(c_ab + cc, cc, sr), col(c_ab + 2 * cc, cc, sr), col(c_ab + 3 * cc, pw, sr),
            state_conv[l].transpose(1, 0, 2), state_pool[l].transpose(1, 0, 2), past, conv_w_l, pool_w_l, pool_s_l)

        q_s = col(c_q, aw, sr).reshape(n_s, N_HEADS, HEAD_DIM)
        k_s = col(c_q + aw, aw, sr).reshape(n_s, N_HEADS, HEAD_DIM)
        v_s = col(c_q + 2 * aw, aw, sr).reshape(n_s, N_HEADS, HEAD_DIM)
        qi_s = (col(c_q + 3 * aw, iw, sr) * (IDX_DIM ** -0.5)).astype(BF16).reshape(n_s, N_IDX_HEADS, IDX_DIM)
        ki_s = col(c_ki, IDX_DIM, sr)
        wi_s = (col(c_ki + IDX_DIM, N_IDX_HEADS, sr) * (N_IDX_HEADS ** -0.5))[:, :, None]
        ki_new_pad = jnp.concatenate([ki_s[:, None, :], jnp.zeros((n_s, PAGE_SIZE - 1, IDX_DIM), F32)], axis=1)
        scores = _sample_scores(l, pt, qi_s, wi_s, ki_new_pad, cache_kidx)
        sel = _sample_select(scores, n_sel_s, n_pages + 1)[:, :, 0]
        dist = jnp.clip(past - sel, 0, MAX_DISTANCE)
        bias_sel = rel_bias[_t5_bucket(dist)][..., None]
        yc_s = _sample_attend(l, pt, sel, q_s * (HEAD_DIM ** -0.5), bias_sel, k_s, v_s, cache_k, cache_v).reshape(n_s, aw)

        pad_rows = t_all - t_p - n_s

        def all_rows(yp, ys_):
            return jnp.concatenate([yp, ys_, jnp.zeros((pad_rows, yp.shape[1]), BF16)], axis=0)

        wr_hi, wr_lo = _split_bf16(w_router[l].T)
        x1, top_e, top_g = _merge(
            alpha, x, z, all_rows(ya_p, ya_s), all_rows(yb_p, yb_s),
            all_rows(yc_p.reshape(t_p, aw), yc_s.astype(BF16)),
            p_a[l].astype(BF16), p_b[l].astype(BF16), p_c[l].astype(BF16), w_o[l].astype(BF16),
            ln1_g[l][None, :], ln1_b[l][None, :], wr_hi, wr_lo, b_router[l][:, None])

        x = _moe(alpha, x1, top_e, top_g, w_gate[l].astype(BF16), b_gate[l][:, None, :], w_up[l].astype(BF16),
                 b_up[l][:, None, :], w_down[l].astype(BF16), b_down[l][:, None, :], ln2_g[l][None, :], ln2_b[l][None, :])

        u_p = (col(c_ab + cc, cc, pr) * col(c_ab + 2 * cc, cc, pr)).reshape(n_b, seq, cc)
        up_p = col(c_ab + 3 * cc, pw, pr).reshape(n_b, seq, pw)
        outs["kp"].append(k_p.reshape(n_b, seq, N_HEADS, HEAD_DIM))
        outs["vp"].append(v_p.reshape(n_b, seq, N_HEADS, HEAD_DIM))
        outs["kip"].append(ki_p)
        outs["cp"].append(u_p[:, seq - (CONV_WIDTH - 1):])
        outs["pp"].append(up_p[:, seq - POOL_STATE:])
        outs["ks"].append(k_s[:, None])
        outs["vs"].append(v_s[:, None])
        outs["kis"].append(ki_s[:, None])
        outs["cs"].append(jnp.concatenate([state_conv[l], u_s[:, None]], axis=1)[:, -(CONV_WIDTH - 1):])
        outs["ps"].append(jnp.concatenate([state_pool[l], col(c_ab + 3 * cc, pw, sr)[:, None]], axis=1)[:, -POOL_STATE:])

    y_prompt = x[:t_p].reshape(n_b, seq, d)
    y_sample = x[t_p:t_p + n_s].reshape(n_s, dec_seq, d)
    stk = {k: jnp.stack(v) for k, v in outs.items()}
    return (y_prompt, y_sample, stk["kp"], stk["vp"], stk["kip"], stk["cp"], stk["pp"],
            stk["ks"], stk["vs"], stk["kis"], stk["cs"], stk["ps"])
```

```python
import functools
import math

import numpy as np
import jax
import jax.numpy as jnp
from jax import lax
from jax.experimental import pallas as pl
from jax.experimental.pallas import tpu as pltpu

F32 = jnp.float32
BF16 = jnp.bfloat16
I32 = jnp.int32

N_HEADS = 8
HEAD_DIM = 64
N_IDX_HEADS = 8
IDX_DIM = 64
TOPK_MAX = 256
POOL_WINDOWS = (2, 4, 8, 16)
POOL_STATE = max(POOL_WINDOWS) - 1
CONV_WIDTH = 3
N_EXPERTS = 32
TOP_K = 4
SWIGLU_LIMIT = 7.0
SWIGLU_ALPHA = 1.702
LN_EPS = 1e-5
NUM_BUCKETS = 32
MAX_DISTANCE = 128
PAGE_SIZE = 128

LANES = 128
SUBLANES = 8
VMEM_LIMIT = 56 * 1024 * 1024

ROW_TILE = 256
ATT_TILE = 256
ATT_PIECES = 2
VA_ONES = 16
MOE_BLOCK = 256
HALO = 16
INT_MIN = np.int32(-2 ** 31)
NEG_INF = float("-inf")


def _cparams(*sem):
    return pltpu.CompilerParams(dimension_semantics=sem, vmem_limit_bytes=VMEM_LIMIT)


def _round_up(n, m):
    return (n + m - 1) // m * m


def _pick_tile(n, candidates):
    for c in candidates:
        if n % c == 0:
            return c
    raise ValueError(f"no tile for {n} in {candidates}")


def _single(block_shape, index_map):
    return pl.BlockSpec(block_shape, index_map, pipeline_mode=pl.Buffered(1))


def _inproj_kernel(x_ref, w_ref, b_ref, o_ref):
    x = x_ref[...].astype(BF16)
    o_ref[...] = jnp.dot(x, w_ref[...], preferred_element_type=F32) + b_ref[...]


def _inproj(x, w, b):
    m, k = x.shape
    n = w.shape[1]
    tm = _pick_tile(m, (1280, 1024, 768, 512, 256))
    tn = _pick_tile(n, (1536, 1280, 1024, 768, 512, 256, 128))
    return pl.pallas_call(
        _inproj_kernel,
        grid=(n // tn, m // tm),
        in_specs=[
            pl.BlockSpec((tm, k), lambda j, i: (i, 0)),
            pl.BlockSpec((k, tn), lambda j, i: (0, j)),
            pl.BlockSpec((1, tn), lambda j, i: (0, j)),
        ],
        out_specs=pl.BlockSpec((tm, tn), lambda j, i: (i, j)),
        out_shape=jax.ShapeDtypeStruct((m, n), F32),
        compiler_params=_cparams("parallel", "parallel"),
        name="inproj",
    )(x, w, b)


def _pool_out(diffs, pw_ref, ps_ref):
    outs = [jnp.dot(d.astype(BF16), pw_ref[g], preferred_element_type=F32) for g, d in enumerate(diffs)]
    return jnp.concatenate(outs, axis=-1) * ps_ref[...]


def _mixer_prompt_kernel(ab_ref, ac_ref, ah_ref, up_ref, cw_ref, pw_ref, ps_ref, ya_ref, yb_ref, eu_ref, ep_ref):
    si = pl.program_id(1)
    ts = ab_ref.shape[0]
    pg = up_ref.shape[1] // len(POOL_WINDOWS)

    @pl.when(si == 0)
    def _():
        eu_ref[0:HALO, :] = jnp.zeros((HALO, eu_ref.shape[1]), F32)
        ep_ref[0:HALO, :] = jnp.zeros((HALO, ep_ref.shape[1]), F32)

    @pl.when(si > 0)
    def _():
        eu_ref[0:HALO, :] = eu_ref[ts:ts + HALO, :]
        ep_ref[0:HALO, :] = ep_ref[ts:ts + HALO, :]

    u = ac_ref[...] * ah_ref[...]
    eu_ref[HALO:HALO + ts, :] = u
    ep_ref[HALO:HALO + ts, :] = up_ref[...]
    conv = cw_ref[0:1, :] * eu_ref[HALO - 2:HALO - 2 + ts, :]
    conv = conv + cw_ref[1:2, :] * eu_ref[HALO - 1:HALO - 1 + ts, :]
    conv = conv + cw_ref[2:3, :] * u
    ya_ref[...] = (ab_ref[...] * conv).astype(ya_ref.dtype)

    pos = si * ts + lax.broadcasted_iota(I32, (ts, 1), 0)
    diffs = []
    for g, w in enumerate(POOL_WINDOWS):
        sl = slice(g * pg, (g + 1) * pg)
        win = ep_ref[HALO:HALO + ts, sl]
        for j in range(1, w):
            win = win + ep_ref[HALO - j:HALO - j + ts, sl]
        cnt = jnp.minimum(w, pos + 1).astype(F32)
        diffs.append(win / cnt - up_ref[:, sl])
    yb_ref[...] = _pool_out(diffs, pw_ref, ps_ref).astype(yb_ref.dtype)


def _mixer_prompt(z, col0, n_b, seq, cc, conv_w, pool_w_bf, pool_scale):
    ts = _pick_tile(seq, (512, 256, 128))
    nblk = seq // ts
    cb = col0 // cc

    def zspec(i):
        return pl.BlockSpec((ts, cc), lambda b, s, i=i: (b * nblk + s, cb + i))

    out_spec = pl.BlockSpec((ts, cc), lambda b, s: (b * nblk + s, 0))
    return pl.pallas_call(
        _mixer_prompt_kernel,
        grid=(n_b, nblk),
        in_specs=[zspec(0), zspec(1), zspec(2), zspec(3),
                  pl.BlockSpec(conv_w.shape, lambda b, s: (0, 0)),
                  pl.BlockSpec(pool_w_bf.shape, lambda b, s: (0, 0, 0)),
                  pl.BlockSpec(pool_scale.shape, lambda b, s: (0, 0))],
        out_specs=[out_spec, out_spec],
        out_shape=[jax.ShapeDtypeStruct((n_b * seq, cc), BF16)] * 2,
        scratch_shapes=[pltpu.VMEM((HALO + ts, cc), F32), pltpu.VMEM((HALO + ts, cc), F32)],
        compiler_params=_cparams("arbitrary", "arbitrary"),
        name="mixer_prompt",
    )(z, z, z, z, conv_w, pool_w_bf, pool_scale)


def _mixer_sample_kernel(pos0, ab_ref, ac_ref, ah_ref, up_ref, cprev_ref, pprev_ref, cw_ref, pw_ref, ps_ref,
                         ya_ref, yb_ref, u_ref):
    pg = up_ref.shape[1] // len(POOL_WINDOWS)
    u = ac_ref[...] * ah_ref[...]
    u_ref[...] = u
    conv = cw_ref[0:1, :] * cprev_ref[0] + cw_ref[1:2, :] * cprev_ref[1] + cw_ref[2:3, :] * u
    ya_ref[...] = (ab_ref[...] * conv).astype(ya_ref.dtype)
    diffs = []
    for g, w in enumerate(POOL_WINDOWS):
        sl = slice(g * pg, (g + 1) * pg)
        win = up_ref[:, sl]
        for j in range(1, w):
            win = win + pprev_ref[POOL_STATE - j, :, sl]
        cnt = float(min(w, pos0 + 1))
        diffs.append(win / cnt - up_ref[:, sl])
    yb_ref[...] = _pool_out(diffs, pw_ref, ps_ref).astype(yb_ref.dtype)


def _mixer_sample(a_b, a_c, a_h, u_pool, conv_prev_t, pool_prev_t, pos0, conv_w, pool_w_bf, pool_scale):
    n, cc = a_b.shape
    return pl.pallas_call(
        functools.partial(_mixer_sample_kernel, pos0),
        out_shape=[jax.ShapeDtypeStruct((n, cc), BF16), jax.ShapeDtypeStruct((n, cc), BF16),
                   jax.ShapeDtypeStruct((n, cc), F32)],
        compiler_params=pltpu.CompilerParams(vmem_limit_bytes=VMEM_LIMIT),
        name="mixer_sample",
    )(a_b, a_c, a_h, u_pool, conv_prev_t, pool_prev_t, conv_w, pool_w_bf, pool_scale)


def _sortable_key(score):
    bits = lax.bitcast_convert_type(score, I32)
    return bits ^ ((bits >> 31) & np.int32(0x7FFFFFFF))


def _kth_largest_key(count_ge, k, n_total, shape):
    def cond(st):
        it, _, cnt_cur = st
        return (it < 32) & jnp.any(cnt_cur != k)

    def body(st):
        it, thr_u, cnt_cur = st
        cand_u = thr_u | jnp.left_shift(jnp.int32(1), 31 - it)
        cnt = count_ge(cand_u ^ INT_MIN)
        ok = cnt >= k
        return it + 1, jnp.where(ok, cand_u, thr_u), jnp.where(ok, cnt, cnt_cur)

    init = (jnp.int32(0), jnp.zeros(shape, I32), jnp.zeros(shape, I32) + n_total)
    _, thr_u, cnt_cur = lax.while_loop(cond, body, init)
    return thr_u ^ INT_MIN, cnt_cur


def _dsa_prompt_kernel(n_sel, qi_t_ref, w_t_ref, kib_ref, q_t_ref, k_ref, va_ref, bias_ref, o_ref,
                       key_scr, mask_scr, qz_scr, acc_scr, m_scr, lg_scr, p_scr):
    qi = pl.program_id(1)
    tq = o_ref.shape[0]
    tk = tq
    n_heads = N_HEADS
    dh = HEAD_DIM

    def score_chunk(c):
        r0 = pl.multiple_of(c * tk, tk)
        kc = kib_ref[pl.ds(r0, tk), :]
        acc = jnp.zeros((tk, tq), F32)
        for h in range(N_IDX_HEADS):
            d = jnp.dot(kc, qi_t_ref[h * IDX_DIM:(h + 1) * IDX_DIM, :], preferred_element_type=F32)
            acc = acc + w_t_ref[h:h + 1, :] * jnp.maximum(d, 0.0)
        return acc

    def score_body(c, carry):
        key_scr[c] = _sortable_key(score_chunk(c))
        return carry

    lax.fori_loop(0, qi, score_body, 0)
    key_pos = lax.broadcasted_iota(I32, (tk, tq), 0)
    qry_pos = lax.broadcasted_iota(I32, (tk, tq), 1)
    diag = jnp.where(key_pos <= qry_pos, score_chunk(qi), NEG_INF)
    key_scr[qi] = _sortable_key(diag)

    def count_where(pred):
        def body(c, acc):
            ind = jnp.where(pred(key_scr[c], c), 1, 0).astype(I32)
            return acc + ind.reshape(tk // SUBLANES, SUBLANES, tq).sum(axis=0)

        acc = lax.fori_loop(0, qi + 1, body, jnp.zeros((SUBLANES, tq), I32))
        return acc.sum(axis=0, keepdims=True)

    thr, cnt_ge = _kth_largest_key(lambda cand: count_where(lambda blk, c: blk >= cand), n_sel,
                                   (qi + 1) * tk, (1, tq))

    neg_inf_key = _sortable_key(jnp.full((1, tq), NEG_INF, F32))
    overflow = (cnt_ge > n_sel) & (thr > neg_inf_key)

    @pl.when(jnp.any(overflow))
    def _():
        need = n_sel - count_where(lambda blk, c: blk > thr)

        def count_tied_below(bound):
            return count_where(lambda blk, c: (blk == thr) & (c * tk + key_pos < bound))

        def bis_body(it, lohi):
            lo, hi = lohi
            mid = (lo + hi) // 2
            ok = count_tied_below(mid) >= need
            return jnp.where(ok, lo, mid), jnp.where(ok, mid, hi)

        n_keys = (qi + 1) * tk
        lo0 = jnp.zeros((1, tq), I32)
        hi0 = jnp.zeros((1, tq), I32) + n_keys
        n_it = int(math.ceil(math.log2(key_scr.shape[0] * tk))) + 1
        _, bound = lax.fori_loop(0, n_it, bis_body, (lo0, hi0))

        def demote(c, carry):
            blk = key_scr[c]
            drop = overflow & (blk == thr) & (c * tk + key_pos >= bound)
            key_scr[c] = jnp.where(drop, thr - 1, blk)
            return carry

        lax.fori_loop(0, qi + 1, demote, 0)

    zero_half = jnp.zeros((dh, tq), BF16)
    for h in range(n_heads):
        qh = q_t_ref[h * dh:(h + 1) * dh, :]
        pair = (qh, zero_half) if h % 2 == 0 else (zero_half, qh)
        qz_scr[h] = jnp.concatenate(pair, axis=0)
    pk = tk // ATT_PIECES
    m_scr[...] = jnp.full(m_scr.shape, -1e30, F32)
    acc_scr[...] = jnp.zeros(acc_scr.shape, F32)

    def attend(c, bias_idx):
        c0 = pl.multiple_of(c * tk, tk)
        mask_scr[...] = jnp.where(key_scr[c] >= thr, 0.0, NEG_INF).astype(F32)

        def logits_step(h):
            p2 = h // 2
            cm = None
            for j in range(ATT_PIECES):
                rows = slice(j * pk, (j + 1) * pk)
                kc = k_ref[pl.ds(pl.multiple_of(c0 + j * pk, pk), pk), p2 * 2 * dh:(p2 + 1) * 2 * dh]
                logit = jnp.dot(kc, qz_scr[h], preferred_element_type=F32) + mask_scr[rows, :]
                if bias_idx is not None:
                    logit = logit + bias_ref[bias_idx, h, rows, :]
                lg_scr[h, rows, :] = logit
                pm = logit.reshape(pk // SUBLANES, SUBLANES, tq).max(axis=0)
                cm = pm if cm is None else jnp.maximum(cm, pm)
            m_old = m_scr[h]
            m_new = jnp.maximum(m_old, jnp.max(cm, axis=0, keepdims=True))
            m_scr[h] = m_new
            return m_old, m_new

        def exp_step(h, m_new):
            p_scr[h] = jnp.exp(lg_scr[h].reshape(tk // SUBLANES, SUBLANES, tq) - m_new[None]
                               ).reshape(tk, tq).astype(BF16)

        def value_step(h, m_old, m_new):
            alpha = jnp.exp(m_old - m_new)[0:1, :]
            pv = jnp.dot(va_ref[h, :, pl.ds(c0, tk)], p_scr[h], preferred_element_type=F32)
            acc_scr[h] = alpha * acc_scr[h] + pv

        stats = logits_step(0)
        for h in range(n_heads):
            nxt = logits_step(h + 1) if h + 1 < n_heads else None
            exp_step(h, stats[1])
            value_step(h, *stats)
            stats = nxt

    def far_body(c, carry):
        attend(c, None)
        return carry

    lax.fori_loop(0, jnp.maximum(qi - 1, 0), far_body, 0)

    @pl.when(qi >= 1)
    def _():
        attend(qi - 1, 1)

    attend(qi, 0)
    rows = [acc_scr[h, 0:dh, :] / acc_scr[h, dh:dh + 1, :] for h in range(n_heads)]
    o_ref[...] = jnp.concatenate(rows, axis=0).T.astype(o_ref.dtype)


def _dsa_prompt(qi_t, w_t, kib, q_t, k_bf, v_aug, bias_tab, n_sel):
    n_b, width, seq = q_t.shape
    tq = ATT_TILE
    nq = seq // tq
    va_rows = v_aug.shape[2]
    kernel = functools.partial(_dsa_prompt_kernel, n_sel)
    return pl.pallas_call(
        kernel,
        grid=(n_b, nq),
        in_specs=[
            pl.BlockSpec((None, qi_t.shape[1], tq), lambda b, q: (b, 0, q)),
            pl.BlockSpec((None, w_t.shape[1], tq), lambda b, q: (b, 0, q)),
            _single((None, seq, kib.shape[2]), lambda b, q: (b, 0, 0)),
            pl.BlockSpec((None, width, tq), lambda b, q: (b, 0, q)),
            _single((None, seq, width), lambda b, q: (b, 0, 0)),
            _single((None, N_HEADS, va_rows, seq), lambda b, q: (b, 0, 0, 0)),
            _single(bias_tab.shape, lambda b, q: (0, 0, 0, 0)),
        ],
        out_specs=pl.BlockSpec((None, tq, width), lambda b, q: (b, q, 0)),
        out_shape=jax.ShapeDtypeStruct((n_b, seq, width), BF16),
        scratch_shapes=[
            pltpu.VMEM((nq, tq, tq), I32),
            pltpu.VMEM((tq, tq), F32),
            pltpu.VMEM((N_HEADS, 2 * HEAD_DIM, tq), BF16),
            pltpu.VMEM((N_HEADS, va_rows, tq), F32),
            pltpu.VMEM((N_HEADS, SUBLANES, tq), F32),
            pltpu.VMEM((N_HEADS, tq, tq), F32),
            pltpu.VMEM((N_HEADS, tq, tq), BF16),
        ],
        compiler_params=_cparams("arbitrary", "arbitrary"),
        name="dsa_prompt",
    )(qi_t, w_t, kib, q_t, k_bf, v_aug, bias_tab)


def _sample_score_kernel(layer, n_pages, pt_ref, qi_ref, w_ref, knew_ref, kcache_ref, o_ref, kbuf, sem):
    b = pl.program_id(0)
    nb = pl.num_programs(0)
    slot = b % 2

    def page_copy(seq_id, p, s):
        return pltpu.make_async_copy(kcache_ref.at[layer, pt_ref[seq_id, p]], kbuf.at[s, p], sem.at[s])

    def start_all(seq_id, s):
        def body(p, c):
            page_copy(seq_id, p, s).start()
            return c

        lax.fori_loop(0, n_pages, body, 0)

    @pl.when(b == 0)
    def _():
        start_all(0, 0)

    @pl.when(b + 1 < nb)
    def _():
        start_all(b + 1, 1 - slot)

    def wait(p, c):
        page_copy(b, p, slot).wait()
        return c

    lax.fori_loop(0, n_pages, wait, 0)

    qi = qi_ref[0]
    w = w_ref[0]

    def score_rows(keys_t):
        d = jnp.dot(qi, keys_t.astype(BF16), preferred_element_type=F32)
        return jnp.sum(w * jnp.maximum(d, 0.0), axis=0, keepdims=True)

    def page_body(p, c):
        o_ref[0, pl.ds(p, 1), :] = score_rows(kbuf[slot, p])
        return c

    lax.fori_loop(0, n_pages, page_body, 0)
    own = score_rows(knew_ref[0])
    lane = lax.broadcasted_iota(I32, own.shape, 1)
    o_ref[0, n_pages:n_pages + 1, :] = jnp.where(lane == 0, own, NEG_INF)
    n_rows = o_ref.shape[1]
    if n_rows > n_pages + 1:
        o_ref[0, n_pages + 1:, :] = jnp.full((n_rows - n_pages - 1, PAGE_SIZE), NEG_INF, F32)


def _sample_scores(layer, page_table, qi_s, w_s, kidx_new_pad, cache_kidx_t):
    n_b, n_pages = page_table.shape
    n_rows = _round_up(n_pages + 1, SUBLANES)
    grid_spec = pltpu.PrefetchScalarGridSpec(
        num_scalar_prefetch=1,
        grid=(n_b,),
        in_specs=[
            pl.BlockSpec((1, N_IDX_HEADS, IDX_DIM), lambda b, pt: (b, 0, 0)),
            pl.BlockSpec((1, N_IDX_HEADS, 1), lambda b, pt: (b, 0, 0)),
            pl.BlockSpec((1, IDX_DIM, PAGE_SIZE), lambda b, pt: (b, 0, 0)),
            pl.BlockSpec(memory_space=pl.ANY),
        ],
        out_specs=pl.BlockSpec((1, n_rows, PAGE_SIZE), lambda b, pt: (b, 0, 0)),
        scratch_shapes=[pltpu.VMEM((2, n_pages, IDX_DIM, PAGE_SIZE), F32), pltpu.SemaphoreType.DMA((2,))],
    )
    return pl.pallas_call(
        functools.partial(_sample_score_kernel, layer, n_pages),
        grid_spec=grid_spec,
        out_shape=jax.ShapeDtypeStruct((n_b, n_rows, PAGE_SIZE), F32),
        compiler_params=_cparams("arbitrary"),
        name="sample_scores",
    )(page_table, qi_s, w_s, kidx_new_pad, cache_kidx_t)


def _sample_select_kernel(n_sel, s_ref, o_ref):
    score = s_ref[0]
    n_rows = score.shape[0]
    key = _sortable_key(score)
    pos = lax.broadcasted_iota(I32, key.shape, 0) * PAGE_SIZE + lax.broadcasted_iota(I32, key.shape, 1)

    def count(pred):
        return jnp.sum(jnp.where(pred, 1, 0).astype(I32), keepdims=True)

    thr, _ = _kth_largest_key(lambda cand: count(key >= cand), n_sel, n_rows * PAGE_SIZE, (1, 1))
    need = n_sel - count(key > thr)

    def bis_body(it, lohi):
        lo, hi = lohi
        mid = (lo + hi) // 2
        ok = count((key == thr) & (pos < mid)) >= need
        return jnp.where(ok, lo, mid), jnp.where(ok, mid, hi)

    n_it = int(math.ceil(math.log2(n_rows * PAGE_SIZE))) + 1
    _, bound = lax.fori_loop(0, n_it, bis_body,
                             (jnp.zeros((1, 1), I32), jnp.full((1, 1), n_rows * PAGE_SIZE, I32)))
    sel = (key > thr) | ((key == thr) & (pos < bound))
    o_ref[0] = jnp.where(sel, 0.0, NEG_INF).astype(F32)


def _sample_select(scores, n_sel):
    n_b, n_rows, _ = scores.shape
    return pl.pallas_call(
        functools.partial(_sample_select_kernel, n_sel),
        grid=(n_b,),
        in_specs=[pl.BlockSpec((1, n_rows, PAGE_SIZE), lambda b: (b, 0, 0))],
        out_specs=pl.BlockSpec((1, n_rows, PAGE_SIZE), lambda b: (b, 0, 0)),
        out_shape=jax.ShapeDtypeStruct((n_b, n_rows, PAGE_SIZE), F32),
        compiler_params=_cparams("arbitrary"),
        name="sample_select",
    )(scores)


def _sample_attend_kernel(layer, n_pages, pt_ref, q3_ref, q2_ref, mask_ref, bias_ref, knew_ref, vnew_ref,
                          kcache_ref, vcache_ref, o_ref, kbuf, vbuf, qb_scr, p_scr, acc_scr, ksem, vsem):
    b = pl.program_id(0)
    ppc = kbuf.shape[1]
    n_chunks = n_pages // ppc

    def page_copy(cache_ref, buf, sem, c, j, s):
        return pltpu.make_async_copy(cache_ref.at[layer, pt_ref[b, c * ppc + j]], buf.at[s, j], sem.at[s])

    def start(cache_ref, buf, sem, c, s):
        for j in range(ppc):
            page_copy(cache_ref, buf, sem, c, j, s).start()

    def wait(cache_ref, buf, sem, c, s):
        for j in range(ppc):
            page_copy(cache_ref, buf, sem, c, j, s).wait()

    start(kcache_ref, kbuf, ksem, 0, 0)
    start(vcache_ref, vbuf, vsem, 0, 0)
    qb_scr[...] = jnp.broadcast_to(q3_ref[0], qb_scr.shape)

    def k_body(c, carry):
        s = c % 2

        @pl.when(c + 1 < n_chunks)
        def _():
            start(kcache_ref, kbuf, ksem, c + 1, 1 - s)

        wait(kcache_ref, kbuf, ksem, c, s)
        for j in range(ppc):
            page = c * ppc + j
            logit = jnp.sum(kbuf[s, j] * qb_scr[...], axis=1)
            p_scr[page] = logit + bias_ref[page] + mask_ref[0, pl.ds(page, 1), :]
        return carry

    lax.fori_loop(0, n_chunks, k_body, 0)
    own = jnp.sum(q2_ref[0] * knew_ref[0], axis=1, keepdims=True)
    lane = lax.broadcasted_iota(I32, (N_HEADS, PAGE_SIZE), 1)
    p_scr[n_pages] = jnp.where(lane == 0, own, NEG_INF) + bias_ref[n_pages] + mask_ref[0, n_pages:n_pages + 1, :]

    logits = p_scr[...]
    m = jnp.max(jnp.max(logits, axis=0), axis=1, keepdims=True)
    p = jnp.exp(logits - m[None])
    denom = jnp.sum(jnp.sum(p, axis=0), axis=1, keepdims=True)
    p_scr[...] = p

    acc_scr[...] = jnp.zeros(acc_scr.shape, F32)

    def v_body(c, carry):
        s = c % 2

        @pl.when(c + 1 < n_chunks)
        def _():
            start(vcache_ref, vbuf, vsem, c + 1, 1 - s)

        wait(vcache_ref, vbuf, vsem, c, s)
        for j in range(ppc):
            acc_scr[...] += p_scr[c * ppc + j][:, None, :] * vbuf[s, j]
        return carry

    lax.fori_loop(0, n_chunks, v_body, 0)
    out = jnp.sum(acc_scr[...], axis=2) + p_scr[n_pages][:, 0:1] * vnew_ref[0]
    o_ref[0] = out / denom


def _sample_attend(layer, page_table, q_s, mask_add, bias_tab, k_new, v_new, cache_k_t, cache_v_t):
    n_b, n_pages = page_table.shape
    n_rows = mask_add.shape[1]
    ppc = _pick_tile(n_pages, (4, 2, 1))
    slab = (N_HEADS, HEAD_DIM, PAGE_SIZE)
    grid_spec = pltpu.PrefetchScalarGridSpec(
        num_scalar_prefetch=1,
        grid=(n_b,),
        in_specs=[
            pl.BlockSpec((1, N_HEADS, HEAD_DIM, 1), lambda b, pt: (b, 0, 0, 0)),
            pl.BlockSpec((1, N_HEADS, HEAD_DIM), lambda b, pt: (b, 0, 0)),
            pl.BlockSpec((1, n_rows, PAGE_SIZE), lambda b, pt: (b, 0, 0)),
            pl.BlockSpec(bias_tab.shape, lambda b, pt: (0, 0, 0)),
            pl.BlockSpec((1, N_HEADS, HEAD_DIM), lambda b, pt: (b, 0, 0)),
            pl.BlockSpec((1, N_HEADS, HEAD_DIM), lambda b, pt: (b, 0, 0)),
            pl.BlockSpec(memory_space=pl.ANY),
            pl.BlockSpec(memory_space=pl.ANY),
        ],
        out_specs=pl.BlockSpec((1, N_HEADS, HEAD_DIM), lambda b, pt: (b, 0, 0)),
        scratch_shapes=[pltpu.VMEM((2, ppc) + slab, F32), pltpu.VMEM((2, ppc) + slab, F32),
                        pltpu.VMEM(slab, F32), pltpu.VMEM((n_pages + 1, N_HEADS, PAGE_SIZE), F32),
                        pltpu.VMEM(slab, F32), pltpu.SemaphoreType.DMA((2,)), pltpu.SemaphoreType.DMA((2,))],
    )
    return pl.pallas_call(
        functools.partial(_sample_attend_kernel, layer, n_pages),
        grid_spec=grid_spec,
        out_shape=jax.ShapeDtypeStruct((n_b, N_HEADS, HEAD_DIM), F32),
        compiler_params=_cparams("arbitrary"),
        name="sample_attend",
    )(page_table, q_s[..., None], q_s, mask_add, bias_tab, k_new, v_new, cache_k_t, cache_v_t)


def _layer_norm(x, g, b):
    mu = jnp.mean(x, axis=-1, keepdims=True)
    xc = x - mu
    var = jnp.mean(xc * xc, axis=-1, keepdims=True)
    return xc * lax.rsqrt(var + LN_EPS) * g + b


def _split_bf16(x):
    hi = x.astype(BF16)
    lo = (x - hi.astype(F32)).astype(BF16)
    return hi, lo


def _merge_kernel(alpha, x_ref, ga_ref, gb_ref, gc_ref, ya_ref, yb_ref, yc_ref, pa_ref, pb_ref, pc_ref, wo_ref,
                  g1_ref, b1_ref, wr_hi_ref, wr_lo_ref, br_ref, x1_ref, x1t_ref, te_ref, tg_ref):
    def branch(g_ref, y_ref, p_ref):
        return jax.nn.sigmoid(g_ref[...]) * jnp.dot(y_ref[...], p_ref[...], preferred_element_type=F32)

    merged = branch(ga_ref, ya_ref, pa_ref) + branch(gb_ref, yb_ref, pb_ref) + branch(gc_ref, yc_ref, pc_ref)
    y = alpha * x_ref[...] + jnp.dot(merged.astype(BF16), wo_ref[...], preferred_element_type=F32)
    x1 = _layer_norm(y, g1_ref[...], b1_ref[...])
    x1_ref[...] = x1
    x1t_ref[...] = x1.reshape(x1t_ref.shape)

    x_hi, x_lo = _split_bf16(x1)
    dims = (((1,), (1,)), ((), ()))
    logits = lax.dot_general(wr_hi_ref[...], x_hi, dims, preferred_element_type=F32)
    logits = logits + lax.dot_general(wr_hi_ref[...], x_lo, dims, preferred_element_type=F32)
    logits = logits + lax.dot_general(wr_lo_ref[...], x_hi, dims, preferred_element_type=F32)
    logits = logits + br_ref[...]
    eidx = lax.broadcasted_iota(I32, logits.shape, 0)
    vals, idxs = [], []
    cur = logits
    for _ in range(TOP_K):
        v = jnp.max(cur, axis=0, keepdims=True)
        i = jnp.min(jnp.where(cur == v, eidx, N_EXPERTS), axis=0, keepdims=True)
        vals.append(v)
        idxs.append(i)
        cur = jnp.where(eidx == i, NEG_INF, cur)
    top_v = jnp.concatenate(vals, axis=0)
    e = jnp.exp(top_v - vals[0])
    tg_ref[...] = e / jnp.sum(e, axis=0, keepdims=True)
    te_ref[...] = jnp.concatenate(idxs, axis=0)


def _merge(alpha, x, z, y_a, y_b, y_c, p_a, p_b, p_c, w_o, ln_g, ln_b, wr_hi, wr_lo, b_r):
    t, d = x.shape
    cc = y_a.shape[1]
    tm = ROW_TILE

    def const(a):
        return pl.BlockSpec(a.shape, lambda i: (0,) * a.ndim)

    def rows(width, col=0):
        return pl.BlockSpec((tm, width), lambda i, col=col: (i, col))

    return pl.pallas_call(
        functools.partial(_merge_kernel, alpha),
        grid=(t // tm,),
        in_specs=[rows(d), rows(d, 0), rows(d, 1), rows(d, 2), rows(cc), rows(cc), rows(cc),
                  const(p_a), const(p_b), const(p_c), const(w_o), const(ln_g), const(ln_b),
                  const(wr_hi), const(wr_lo), const(b_r)],
        out_specs=[rows(d), pl.BlockSpec((tm, d // LANES, LANES), lambda i: (i, 0, 0)),
                   pl.BlockSpec((TOP_K, tm), lambda i: (0, i)), pl.BlockSpec((TOP_K, tm), lambda i: (0, i))],
        out_shape=[jax.ShapeDtypeStruct((t, d), F32), jax.ShapeDtypeStruct((t, d // LANES, LANES), F32),
                   jax.ShapeDtypeStruct((TOP_K, t), I32), jax.ShapeDtypeStruct((TOP_K, t), F32)],
        compiler_params=_cparams("parallel"),
        name="merge_ln_router",
    )(x, z, z, z, y_a, y_b, y_c, p_a, p_b, p_c, w_o, ln_g, ln_b, wr_hi, wr_lo, b_r)


class _RowGather:
    def __init__(self, idx_ref, src_ref, buf, idx_smem, isem, dsem, rows):
        self.idx_ref, self.src_ref, self.buf = idx_ref, src_ref, buf
        self.idx_smem, self.isem, self.dsem, self.rows = idx_smem, isem, dsem, rows

    def _idx_copy(self, step):
        s = step % 2
        return pltpu.make_async_copy(self.idx_ref.at[pl.ds(step * SUBLANES, SUBLANES), :], self.idx_smem.at[s],
                                     self.isem.at[s])

    def _row_copy(self, token, s, r):
        return pltpu.make_async_copy(self.src_ref.at[token], self.buf.at[s, r], self.dsem.at[s])

    def _issue(self, step):
        s = step % 2

        def body(r, c):
            self._row_copy(self.idx_smem[s, r // LANES, r % LANES], s, r).start()
            return c

        lax.fori_loop(0, self.rows, body, 0)

    def advance(self, i, n_active):
        @pl.when(i == 0)
        def _():
            self._idx_copy(0).start()
            self._idx_copy(0).wait()
            self._issue(0)

            @pl.when(n_active > 1)
            def _():
                self._idx_copy(1).start()

        @pl.when(i + 1 < n_active)
        def _():
            self._idx_copy(i + 1).wait()
            self._issue(i + 1)

        @pl.when(i + 2 < n_active)
        def _():
            self._idx_copy(i + 2).start()

    def wait(self, i):
        s = i % 2

        def body(r, c):
            self._row_copy(0, s, r).wait()
            return c

        lax.fori_loop(0, self.rows, body, 0)


def _moe_kernel(blk_e_ref, n_used_ref, idx_ref, x_hbm_ref, rg_ref, wg_ref, bg_ref, wu_ref, bu_ref, wd_ref, bd_ref,
                o_ref, xbuf, idx_smem, isem, dsem):
    i = pl.program_id(0)
    n_used = n_used_ref[0]
    bm = xbuf.shape[1]
    gather = _RowGather(idx_ref, x_hbm_ref, xbuf, idx_smem, isem, dsem, bm)

    @pl.when(i < n_used)
    def _():
        gather.advance(i, n_used)
        gather.wait(i)
        xe = xbuf[i % 2].reshape(bm, wg_ref.shape[1]).astype(BF16)
        g = jnp.minimum(jnp.dot(xe, wg_ref[0], preferred_element_type=F32) + bg_ref[0], SWIGLU_LIMIT)
        up = jnp.clip(jnp.dot(xe, wu_ref[0], preferred_element_type=F32) + bu_ref[0], -SWIGLU_LIMIT, SWIGLU_LIMIT)
        h = g * jax.nn.sigmoid(SWIGLU_ALPHA * g) * (up + 1.0)
        y = (jnp.dot(h.astype(BF16), wd_ref[0], preferred_element_type=F32) + bd_ref[0]) * rg_ref[...]
        o_ref[...] = y.reshape(o_ref.shape)

    @pl.when(i >= n_used)
    def _():
        o_ref[...] = jnp.zeros(o_ref.shape, o_ref.dtype)


def _moe_experts(blk_e, n_used, idx_tiles, x1t, row_gate, w_gate, b_gate, w_up, b_up, w_down, b_down):
    n_blocks = blk_e.shape[0]
    bm = MOE_BLOCK
    tile = x1t.shape[1:]
    d = tile[0] * tile[1]
    ff = w_gate.shape[2]

    def wspec(shape):
        return pl.BlockSpec((1,) + shape, lambda i, be, nu: (be[i], 0, 0))

    grid_spec = pltpu.PrefetchScalarGridSpec(
        num_scalar_prefetch=2,
        grid=(n_blocks,),
        in_specs=[pl.BlockSpec(memory_space=pl.ANY), pl.BlockSpec(memory_space=pl.ANY),
                  pl.BlockSpec((bm, 1), lambda i, be, nu: (i, 0)),
                  wspec((d, ff)), wspec((1, ff)), wspec((d, ff)), wspec((1, ff)), wspec((ff, d)), wspec((1, d))],
        out_specs=pl.BlockSpec((bm,) + tile, lambda i, be, nu: (i, 0, 0)),
        scratch_shapes=[pltpu.VMEM((2, bm) + tile, F32), pltpu.SMEM((2, SUBLANES, LANES), I32),
                        pltpu.SemaphoreType.DMA((2,)), pltpu.SemaphoreType.DMA((2,))],
    )
    return pl.pallas_call(
        _moe_kernel,
        grid_spec=grid_spec,
        out_shape=jax.ShapeDtypeStruct((n_blocks * bm,) + tile, F32),
        compiler_params=_cparams("arbitrary"),
        name="moe_experts",
    )(blk_e, n_used, idx_tiles, x1t, row_gate, w_gate, b_gate, w_up, b_up, w_down, b_down)


def _combine_kernel(alpha, idx_ref, x_ref, y_hbm_ref, lg_ref, lb_ref, o_ref, ybuf, idx_smem, isem, dsem):
    i = pl.program_id(0)
    tm = x_ref.shape[0]
    gather = _RowGather(idx_ref, y_hbm_ref, ybuf, idx_smem, isem, dsem, TOP_K * tm)
    gather.advance(i, pl.num_programs(0))
    gather.wait(i)
    slot = i % 2
    h = ybuf[slot, 0:tm]
    for k in range(1, TOP_K):
        h = h + ybuf[slot, k * tm:(k + 1) * tm]
    o_ref[...] = _layer_norm(alpha * x_ref[...] + h.reshape(x_ref.shape), lg_ref[...], lb_ref[...])


def _combine(alpha, x1, ys_t, idx_tiles, ln_g, ln_b):
    t, d = x1.shape
    tm = ROW_TILE
    assert TOP_K * tm == SUBLANES * LANES
    tile = ys_t.shape[1:]
    return pl.pallas_call(
        functools.partial(_combine_kernel, alpha),
        grid=(t // tm,),
        in_specs=[pl.BlockSpec(memory_space=pl.ANY),
                  pl.BlockSpec((tm, d), lambda i: (i, 0)),
                  pl.BlockSpec(memory_space=pl.ANY),
                  pl.BlockSpec(ln_g.shape, lambda i: (0, 0)),
                  pl.BlockSpec(ln_b.shape, lambda i: (0, 0))],
        out_specs=pl.BlockSpec((tm, d), lambda i: (i, 0)),
        out_shape=jax.ShapeDtypeStruct((t, d), F32),
        scratch_shapes=[pltpu.VMEM((2, TOP_K * tm) + tile, F32), pltpu.SMEM((2, SUBLANES, LANES), I32),
                        pltpu.SemaphoreType.DMA((2,)), pltpu.SemaphoreType.DMA((2,))],
        compiler_params=_cparams("arbitrary"),
        name="combine_ln",
    )(idx_tiles, x1, ys_t, ln_g, ln_b)


def _moe(alpha, x1, x1t, top_e, top_g, w_gate, b_gate, w_up, b_up, w_down, b_down, ln_g, ln_b):
    t, d = x1.shape
    bm = MOE_BLOCK
    n_assign = t * TOP_K
    flat_e = top_e.T.reshape(-1)
    iota = jnp.arange(n_assign, dtype=I32)
    se, order, sg = lax.sort((flat_e, iota, top_g.T.reshape(-1)), num_keys=1, is_stable=True)
    experts = jnp.arange(N_EXPERTS, dtype=I32)
    is_e = se[:, None] == experts[None, :]
    counts = jnp.sum(is_e.astype(I32), axis=0)
    padded = (counts + bm - 1) // bm * bm
    pad_end = jnp.cumsum(padded)
    pad_start = pad_end - padded
    grp_start = jnp.cumsum(counts) - counts
    dest_sorted = iota + jnp.sum(jnp.where(is_e, (pad_start - grp_start)[None, :], 0), axis=1)
    _, pos = lax.sort((order, dest_sorted), num_keys=1)
    n_blocks = -(-n_assign // bm) + N_EXPERTS
    blk_start = jnp.arange(n_blocks, dtype=I32) * bm
    blk_e = jnp.minimum(jnp.sum((pad_end[None, :] <= blk_start[:, None]).astype(I32), axis=1), N_EXPERTS - 1)
    n_used = (pad_end[-1:] // bm).astype(I32)
    off = blk_start - pad_start[blk_e]
    first_sorted = jnp.clip(grp_start[blk_e] + off, 0, n_assign)
    live = jnp.clip(counts[blk_e] - off, 0, bm)
    take = jax.vmap(lambda a, s: lax.dynamic_slice(a, (s,), (bm,)), in_axes=(None, 0))
    valid = jnp.arange(bm, dtype=I32)[None, :] < live[:, None]
    row_token = jnp.where(valid, take(jnp.pad(order, (0, bm)), first_sorted) // TOP_K, 0)
    row_gate = jnp.where(valid, take(jnp.pad(sg, (0, bm)), first_sorted), 0.0).reshape(n_blocks * bm, 1)

    per_tile = SUBLANES * LANES
    tok_tiles = jnp.pad(row_token.reshape(n_blocks, bm), ((0, 0), (0, per_tile - bm))).reshape(-1, LANES)
    pos_tiles = pos.reshape(t // ROW_TILE, ROW_TILE, TOP_K).transpose(0, 2, 1).reshape(-1, LANES)
    ys_t = _moe_experts(blk_e, n_used, tok_tiles, x1t, row_gate, w_gate, b_gate, w_up, b_up, w_down, b_down)
    return _combine(alpha, x1, ys_t, pos_tiles, ln_g, ln_b)


def _t5_bucket(dist):
    exact = NUM_BUCKETS // 2
    d = jnp.maximum(dist, 1).astype(F32)
    large = exact + (jnp.log(d / exact) / math.log(MAX_DISTANCE / exact) * (NUM_BUCKETS - exact)).astype(I32)
    return jnp.where(dist < exact, dist, jnp.minimum(large, NUM_BUCKETS - 1))


def _prompt_bias_tables(rel_bias, tile):
    far = rel_bias[NUM_BUCKETS - 1]
    by_dist = rel_bias[_t5_bucket(jnp.arange(2 * tile, dtype=I32))] - far[None, :]
    j = jnp.arange(tile, dtype=I32)[:, None]
    i = jnp.arange(tile, dtype=I32)[None, :]
    d0 = i - j
    t0 = jnp.where((d0 >= 0)[..., None], by_dist[jnp.maximum(d0, 0)], NEG_INF)
    t1 = by_dist[tile + i - j]
    return jnp.stack([t0, t1]).transpose(0, 3, 1, 2).astype(F32)


def kernel(x_prompt, x_sample, cache_k, cache_v, cache_kidx, state_conv, state_pool, page_table, rel_bias,
           w_in, b_in, conv_w, pool_w, pool_scale, p_a, p_b, p_c, w_o, ln1_g, ln1_b,
           w_router, b_router, w_gate, b_gate, w_up, b_up, w_down, b_down, ln2_g, ln2_b):
    n_b, seq, d = x_prompt.shape
    n_s, dec_seq, _ = x_sample.shape
    assert dec_seq == 1
    depth = w_in.shape[0]
    n_pages = page_table.shape[1]
    past = n_pages * PAGE_SIZE
    cc = conv_w.shape[2]
    pw = pool_scale.shape[1]
    aw = N_HEADS * HEAD_DIM
    iw = N_IDX_HEADS * IDX_DIM
    assert cc == pw == aw == iw and seq % ATT_TILE == 0 and d % cc == 0
    alpha = float((2 * depth) ** 0.25)
    n_sel_p = min(TOPK_MAX, seq // 4)
    n_sel_s = min(TOPK_MAX, (past + dec_seq) // 4)

    t_p = n_b * seq
    t_all = _round_up(t_p + n_s, ROW_TILE)
    x = jnp.concatenate([x_prompt.reshape(t_p, d), x_sample.reshape(n_s, d),
                         jnp.zeros((t_all - t_p - n_s, d), F32)], axis=0)

    sizes = (cc, cc, cc, pw, aw, aw, aw, iw, IDX_DIM, N_IDX_HEADS, 3 * d)
    n_in = sum(sizes)
    off_g = n_in - 3 * d
    nz = _round_up(n_in, 1536)
    c_ab = 3 * d
    c_q = c_ab + 4 * cc
    c_ki = c_q + 4 * cc
    bias_tab = _prompt_bias_tables(rel_bias, ATT_TILE)
    pt = page_table.astype(I32)
    key_pos = jnp.arange((n_pages + 1) * PAGE_SIZE, dtype=I32)
    bias_tab_s = rel_bias[_t5_bucket(jnp.clip(past - key_pos, 0, MAX_DISTANCE))]
    bias_tab_s = bias_tab_s.reshape(n_pages + 1, PAGE_SIZE, N_HEADS).transpose(0, 2, 1)
    cache_kidx_t = cache_kidx.transpose(0, 1, 3, 2)
    cache_k_t = cache_k.transpose(0, 1, 3, 4, 2)
    cache_v_t = cache_v.transpose(0, 1, 3, 4, 2)

    outs = {name: [] for name in ("kp", "vp", "kip", "cp", "pp", "ks", "vs", "kis", "cs", "ps")}
    for l in range(depth):
        w_l = jnp.concatenate([w_in[l][:, off_g:], w_in[l][:, :off_g], jnp.zeros((d, nz - n_in), F32)], axis=1)
        b_l = jnp.concatenate([b_in[l][off_g:], b_in[l][:off_g], jnp.zeros((nz - n_in,), F32)])[None, :]
        z = _inproj(x, w_l.astype(BF16), b_l)

        def col(c0, width, rows=slice(None)):
            return z[rows, c0:c0 + width]

        pr = slice(0, t_p)
        sr = slice(t_p, t_p + n_s)
        conv_w_l = conv_w[l]
        pool_w_l = pool_w[l].astype(BF16)
        pool_s_l = pool_scale[l][None, :]

        ya_p, yb_p = _mixer_prompt(z, c_ab, n_b, seq, cc, conv_w_l, pool_w_l, pool_s_l)

        q_p = col(c_q, aw, pr).reshape(n_b, seq, aw)
        k_p = col(c_q + aw, aw, pr).reshape(n_b, seq, aw)
        v_p = col(c_q + 2 * aw, aw, pr).reshape(n_b, seq, aw)
        qi_p = col(c_q + 3 * aw, iw, pr).reshape(n_b, seq, iw)
        ki_p = col(c_ki, IDX_DIM, pr).reshape(n_b, seq, IDX_DIM)
        wi_p = col(c_ki + IDX_DIM, N_IDX_HEADS, pr).reshape(n_b, seq, N_IDX_HEADS)
        qi_t = (qi_p * (IDX_DIM ** -0.5)).astype(BF16).transpose(0, 2, 1)
        w_t = (wi_p * (N_IDX_HEADS ** -0.5)).transpose(0, 2, 1)
        q_t = (q_p * (HEAD_DIM ** -0.5)).astype(BF16).transpose(0, 2, 1)
        v_heads_t = v_p.astype(BF16).reshape(n_b, seq, N_HEADS, HEAD_DIM).transpose(0, 2, 3, 1)
        v_aug = jnp.concatenate([v_heads_t, jnp.ones((n_b, N_HEADS, VA_ONES, seq), BF16)], axis=2)
        yc_p = _dsa_prompt(qi_t, w_t, ki_p.astype(BF16), q_t, k_p.astype(BF16), v_aug, bias_tab, n_sel_p)

        ya_s, yb_s, u_s = _mixer_sample(
            col(c_ab, cc, sr), col(c_ab + cc, cc, sr), col(c_ab + 2 * cc, cc, sr), col(c_ab + 3 * cc, pw, sr),
            state_conv[l].transpose(1, 0, 2), state_pool[l].transpose(1, 0, 2), past, conv_w_l, pool_w_l, pool_s_l)

        q_s = col(c_q, aw, sr).reshape(n_s, N_HEADS, HEAD_DIM)
        k_s = col(c_q + aw, aw, sr).reshape(n_s, N_HEADS, HEAD_DIM)
        v_s = col(c_q + 2 * aw, aw, sr).reshape(n_s, N_HEADS, HEAD_DIM)
        qi_s = (col(c_q + 3 * aw, iw, sr) * (IDX_DIM ** -0.5)).astype(BF16).reshape(n_s, N_IDX_HEADS, IDX_DIM)
        ki_s = col(c_ki, IDX_DIM, sr)
        wi_s = (col(c_ki + IDX_DIM, N_IDX_HEADS, sr) * (N_IDX_HEADS ** -0.5))[:, :, None]
        ki_new_pad = jnp.pad(ki_s[:, :, None], ((0, 0), (0, 0), (0, PAGE_SIZE - 1)))
        scores = _sample_scores(l, pt, qi_s, wi_s, ki_new_pad, cache_kidx_t)
        mask_add = _sample_select(scores, n_sel_s)
        yc_s = _sample_attend(l, pt, q_s * (HEAD_DIM ** -0.5), mask_add, bias_tab_s, k_s, v_s,
                              cache_k_t, cache_v_t).reshape(n_s, aw)

        pad_rows = t_all - t_p - n_s

        def all_rows(yp, ys_):
            return jnp.concatenate([yp, ys_, jnp.zeros((pad_rows, yp.shape[1]), BF16)], axis=0)

        wr_hi, wr_lo = _split_bf16(w_router[l].T)
        x1, x1t, top_e, top_g = _merge(
            alpha, x, z, all_rows(ya_p, ya_s), all_rows(yb_p, yb_s),
            all_rows(yc_p.reshape(t_p, aw), yc_s.astype(BF16)),
            p_a[l].astype(BF16), p_b[l].astype(BF16), p_c[l].astype(BF16), w_o[l].astype(BF16),
            ln1_g[l][None, :], ln1_b[l][None, :], wr_hi, wr_lo, b_router[l][:, None])

        x = _moe(alpha, x1, x1t, top_e, top_g, w_gate[l].astype(BF16), b_gate[l][:, None, :], w_up[l].astype(BF16),
                 b_up[l][:, None, :], w_down[l].astype(BF16), b_down[l][:, None, :], ln2_g[l][None, :], ln2_b[l][None, :])

        u_p = (col(c_ab + cc, cc, pr) * col(c_ab + 2 * cc, cc, pr)).reshape(n_b, seq, cc)
        up_p = col(c_ab + 3 * cc, pw, pr).reshape(n_b, seq, pw)
        outs["kp"].append(k_p.reshape(n_b, seq, N_HEADS, HEAD_DIM))
        outs["vp"].append(v_p.reshape(n_b, seq, N_HEADS, HEAD_DIM))
        outs["kip"].append(ki_p)
        outs["cp"].append(u_p[:, seq - (CONV_WIDTH - 1):])
        outs["pp"].append(up_p[:, seq - POOL_STATE:])
        outs["ks"].append(k_s[:, None])
        outs["vs"].append(v_s[:, None])
        outs["kis"].append(ki_s[:, None])
        outs["cs"].append(jnp.concatenate([state_conv[l], u_s[:, None]], axis=1)[:, -(CONV_WIDTH - 1):])
        outs["ps"].append(jnp.concatenate([state_pool[l], col(c_ab + 3 * cc, pw, sr)[:, None]], axis=1)[:, -POOL_STATE:])

    y_prompt = x[:t_p].reshape(n_b, seq, d)
    y_sample = x[t_p:t_p + n_s].reshape(n_s, dec_seq, d)
    stk = {k: jnp.stack(v) for k, v in outs.items()}
    return (y_prompt, y_sample, stk["kp"], stk["vp"], stk["kip"], stk["cp"], stk["pp"],
            stk["ks"], stk["vs"], stk["kis"], stk["cs"], stk["ps"])
```

```python
import functools
import math

import numpy as np
import jax
import jax.numpy as jnp
from jax import lax
from jax.experimental import pallas as pl
from jax.experimental.pallas import tpu as pltpu

F32 = jnp.float32
BF16 = jnp.bfloat16
I32 = jnp.int32

N_HEADS = 8
HEAD_DIM = 64
N_IDX_HEADS = 8
IDX_DIM = 64
TOPK_MAX = 256
POOL_WINDOWS = (2, 4, 8, 16)
POOL_STATE = max(POOL_WINDOWS) - 1
CONV_WIDTH = 3
N_EXPERTS = 32
TOP_K = 4
SWIGLU_LIMIT = 7.0
SWIGLU_ALPHA = 1.702
LN_EPS = 1e-5
NUM_BUCKETS = 32
MAX_DISTANCE = 128
PAGE_SIZE = 128

LANES = 128
SUBLANES = 8
VMEM_LIMIT = 56 * 1024 * 1024

ROW_TILE = 256
ATT_TILE = 256
ATT_PIECES = 2
VA_ONES = 16
MOE_BLOCK = 256
COMBINE_TILE = 64
HALO = 16
INT_MIN = np.int32(-2 ** 31)
LOG2_E = math.log2(math.e)
NEG_INF = float("-inf")


def _cparams(*sem):
    return pltpu.CompilerParams(dimension_semantics=sem, vmem_limit_bytes=VMEM_LIMIT)


def _round_up(n, m):
    return (n + m - 1) // m * m


def _pick_tile(n, candidates):
    for c in candidates:
        if n % c == 0:
            return c
    raise ValueError(f"no tile for {n} in {candidates}")


def _single(block_shape, index_map):
    return pl.BlockSpec(block_shape, index_map, pipeline_mode=pl.Buffered(1))


def _inproj_kernel(x_ref, w_ref, b_ref, o_ref):
    x = x_ref[...].astype(BF16)
    o_ref[...] = jnp.dot(x, w_ref[...], preferred_element_type=F32) + b_ref[...]


def _inproj(x, w, b):
    m, k = x.shape
    n = w.shape[1]
    tm = _pick_tile(m, (1280, 1024, 768, 512, 256))
    tn = _pick_tile(n, (1536, 1280, 1024, 768, 512, 256, 128))
    return pl.pallas_call(
        _inproj_kernel,
        grid=(n // tn, m // tm),
        in_specs=[
            pl.BlockSpec((tm, k), lambda j, i: (i, 0)),
            pl.BlockSpec((k, tn), lambda j, i: (0, j)),
            pl.BlockSpec((1, tn), lambda j, i: (0, j)),
        ],
        out_specs=pl.BlockSpec((tm, tn), lambda j, i: (i, j)),
        out_shape=jax.ShapeDtypeStruct((m, n), F32),
        compiler_params=_cparams("parallel", "parallel"),
        name="inproj",
    )(x, w, b)


def _pool_out(diffs, pw_ref, ps_ref):
    outs = [jnp.dot(d.astype(BF16), pw_ref[g], preferred_element_type=F32) for g, d in enumerate(diffs)]
    return jnp.concatenate(outs, axis=-1) * ps_ref[...]


def _mixer_prompt_kernel(ab_ref, ac_ref, ah_ref, up_ref, cw_ref, pw_ref, ps_ref, ya_ref, yb_ref, eu_ref, ep_ref):
    si = pl.program_id(1)
    ts = ab_ref.shape[0]
    pg = up_ref.shape[1] // len(POOL_WINDOWS)

    @pl.when(si == 0)
    def _():
        eu_ref[0:HALO, :] = jnp.zeros((HALO, eu_ref.shape[1]), F32)
        ep_ref[0:HALO, :] = jnp.zeros((HALO, ep_ref.shape[1]), F32)

    @pl.when(si > 0)
    def _():
        eu_ref[0:HALO, :] = eu_ref[ts:ts + HALO, :]
        ep_ref[0:HALO, :] = ep_ref[ts:ts + HALO, :]

    u = ac_ref[...] * ah_ref[...]
    eu_ref[HALO:HALO + ts, :] = u
    ep_ref[HALO:HALO + ts, :] = up_ref[...]
    conv = cw_ref[0:1, :] * eu_ref[HALO - 2:HALO - 2 + ts, :]
    conv = conv + cw_ref[1:2, :] * eu_ref[HALO - 1:HALO - 1 + ts, :]
    conv = conv + cw_ref[2:3, :] * u
    ya_ref[...] = (ab_ref[...] * conv).astype(ya_ref.dtype)

    pos = si * ts + lax.broadcasted_iota(I32, (ts, 1), 0)
    diffs = []
    for g, w in enumerate(POOL_WINDOWS):
        sl = slice(g * pg, (g + 1) * pg)
        win = ep_ref[HALO:HALO + ts, sl]
        for j in range(1, w):
            win = win + ep_ref[HALO - j:HALO - j + ts, sl]
        cnt = jnp.minimum(w, pos + 1).astype(F32)
        diffs.append(win / cnt - up_ref[:, sl])
    yb_ref[...] = _pool_out(diffs, pw_ref, ps_ref).astype(yb_ref.dtype)


def _mixer_prompt(z, col0, n_b, seq, cc, conv_w, pool_w_bf, pool_scale):
    ts = _pick_tile(seq, (512, 256, 128))
    nblk = seq // ts
    cb = col0 // cc

    def zspec(i):
        return pl.BlockSpec((ts, cc), lambda b, s, i=i: (b * nblk + s, cb + i))

    out_spec = pl.BlockSpec((ts, cc), lambda b, s: (b * nblk + s, 0))
    return pl.pallas_call(
        _mixer_prompt_kernel,
        grid=(n_b, nblk),
        in_specs=[zspec(0), zspec(1), zspec(2), zspec(3),
                  pl.BlockSpec(conv_w.shape, lambda b, s: (0, 0)),
                  pl.BlockSpec(pool_w_bf.shape, lambda b, s: (0, 0, 0)),
                  pl.BlockSpec(pool_scale.shape, lambda b, s: (0, 0))],
        out_specs=[out_spec, out_spec],
        out_shape=[jax.ShapeDtypeStruct((n_b * seq, cc), BF16)] * 2,
        scratch_shapes=[pltpu.VMEM((HALO + ts, cc), F32), pltpu.VMEM((HALO + ts, cc), F32)],
        compiler_params=_cparams("arbitrary", "arbitrary"),
        name="mixer_prompt",
    )(z, z, z, z, conv_w, pool_w_bf, pool_scale)


def _mixer_sample_kernel(pos0, ab_ref, ac_ref, ah_ref, up_ref, cprev_ref, pprev_ref, cw_ref, pw_ref, ps_ref,
                         ya_ref, yb_ref, u_ref):
    pg = up_ref.shape[1] // len(POOL_WINDOWS)
    u = ac_ref[...] * ah_ref[...]
    u_ref[...] = u
    conv = cw_ref[0:1, :] * cprev_ref[0] + cw_ref[1:2, :] * cprev_ref[1] + cw_ref[2:3, :] * u
    ya_ref[...] = (ab_ref[...] * conv).astype(ya_ref.dtype)
    diffs = []
    for g, w in enumerate(POOL_WINDOWS):
        sl = slice(g * pg, (g + 1) * pg)
        win = up_ref[:, sl]
        for j in range(1, w):
            win = win + pprev_ref[POOL_STATE - j, :, sl]
        cnt = float(min(w, pos0 + 1))
        diffs.append(win / cnt - up_ref[:, sl])
    yb_ref[...] = _pool_out(diffs, pw_ref, ps_ref).astype(yb_ref.dtype)


def _mixer_sample(a_b, a_c, a_h, u_pool, conv_prev_t, pool_prev_t, pos0, conv_w, pool_w_bf, pool_scale):
    n, cc = a_b.shape
    return pl.pallas_call(
        functools.partial(_mixer_sample_kernel, pos0),
        out_shape=[jax.ShapeDtypeStruct((n, cc), BF16), jax.ShapeDtypeStruct((n, cc), BF16),
                   jax.ShapeDtypeStruct((n, cc), F32)],
        compiler_params=pltpu.CompilerParams(vmem_limit_bytes=VMEM_LIMIT),
        name="mixer_sample",
    )(a_b, a_c, a_h, u_pool, conv_prev_t, pool_prev_t, conv_w, pool_w_bf, pool_scale)


def _sortable_key(score):
    bits = lax.bitcast_convert_type(score, I32)
    return bits ^ ((bits >> 31) & np.int32(0x7FFFFFFF))


def _kth_largest_key(count_ge, k, n_total, shape):
    def cond(st):
        it, _, cnt_cur = st
        return (it < 32) & jnp.any(cnt_cur != k)

    def body(st):
        it, thr_u, cnt_cur = st
        cand_u = thr_u | jnp.left_shift(jnp.int32(1), 31 - it)
        cnt = count_ge(cand_u ^ INT_MIN)
        ok = cnt >= k
        return it + 1, jnp.where(ok, cand_u, thr_u), jnp.where(ok, cnt, cnt_cur)

    init = (jnp.int32(0), jnp.zeros(shape, I32), jnp.zeros(shape, I32) + n_total)
    _, thr_u, cnt_cur = lax.while_loop(cond, body, init)
    return thr_u ^ INT_MIN, cnt_cur


def _dsa_prompt_kernel(n_sel, qi_t_ref, w_t_ref, kib_ref, q_t_ref, k_ref, va_ref, bias_ref, o_ref,
                       key_scr, mask_scr, qz_scr, acc_scr, m_scr, alpha_scr, lg_scr, p_scr):
    qi = pl.program_id(1)
    tq = o_ref.shape[0]
    tk = tq
    n_heads = N_HEADS
    dh = HEAD_DIM

    def score_chunk(c):
        r0 = pl.multiple_of(c * tk, tk)
        kc = kib_ref[pl.ds(r0, tk), :]
        acc = jnp.zeros((tk, tq), F32)
        for h in range(N_IDX_HEADS):
            d = jnp.dot(kc, qi_t_ref[h * IDX_DIM:(h + 1) * IDX_DIM, :], preferred_element_type=F32)
            acc = acc + w_t_ref[h:h + 1, :] * jnp.maximum(d, 0.0)
        return acc

    def score_body(c, carry):
        key_scr[c] = _sortable_key(score_chunk(c))
        return carry

    lax.fori_loop(0, qi, score_body, 0)
    key_pos = lax.broadcasted_iota(I32, (tk, tq), 0)
    qry_pos = lax.broadcasted_iota(I32, (tk, tq), 1)
    diag = jnp.where(key_pos <= qry_pos, score_chunk(qi), NEG_INF)
    key_scr[qi] = _sortable_key(diag)

    def count_where(pred):
        def body(c, acc):
            ind = jnp.where(pred(key_scr[c], c), 1, 0).astype(I32)
            return acc + ind.reshape(tk // SUBLANES, SUBLANES, tq).sum(axis=0)

        acc = lax.fori_loop(0, qi + 1, body, jnp.zeros((SUBLANES, tq), I32))
        return acc.sum(axis=0, keepdims=True)

    thr, cnt_ge = _kth_largest_key(lambda cand: count_where(lambda blk, c: blk >= cand), n_sel,
                                   (qi + 1) * tk, (1, tq))

    neg_inf_key = _sortable_key(jnp.full((1, tq), NEG_INF, F32))
    overflow = (cnt_ge > n_sel) & (thr > neg_inf_key)

    @pl.when(jnp.any(overflow))
    def _():
        need = n_sel - count_where(lambda blk, c: blk > thr)

        def count_tied_below(bound):
            return count_where(lambda blk, c: (blk == thr) & (c * tk + key_pos < bound))

        def bis_body(it, lohi):
            lo, hi = lohi
            mid = (lo + hi) // 2
            ok = count_tied_below(mid) >= need
            return jnp.where(ok, lo, mid), jnp.where(ok, mid, hi)

        n_keys = (qi + 1) * tk
        lo0 = jnp.zeros((1, tq), I32)
        hi0 = jnp.zeros((1, tq), I32) + n_keys
        n_it = int(math.ceil(math.log2(key_scr.shape[0] * tk))) + 1
        _, bound = lax.fori_loop(0, n_it, bis_body, (lo0, hi0))

        def demote(c, carry):
            blk = key_scr[c]
            drop = overflow & (blk == thr) & (c * tk + key_pos >= bound)
            key_scr[c] = jnp.where(drop, thr - 1, blk)
            return carry

        lax.fori_loop(0, qi + 1, demote, 0)

    zero_half = jnp.zeros((dh, tq), BF16)
    for h in range(n_heads):
        qh = q_t_ref[h * dh:(h + 1) * dh, :]
        pair = (qh, zero_half) if h % 2 == 0 else (zero_half, qh)
        qz_scr[h] = jnp.concatenate(pair, axis=0)
    pk = tk // ATT_PIECES
    m_scr[...] = jnp.full(m_scr.shape, -1e30, F32)
    acc_scr[...] = jnp.zeros(acc_scr.shape, F32)

    def attend(c, bias_idx):
        c0 = pl.multiple_of(c * tk, tk)
        mask_scr[...] = jnp.where(key_scr[c] >= thr, 0.0, NEG_INF).astype(F32)

        def logits_piece(h, j):
            p2 = h // 2
            rows = slice(j * pk, (j + 1) * pk)
            kc = k_ref[pl.ds(pl.multiple_of(c0 + j * pk, pk), pk), p2 * 2 * dh:(p2 + 1) * 2 * dh]
            logit = jnp.dot(kc, qz_scr[h], preferred_element_type=F32) + mask_scr[rows, :]
            if bias_idx is not None:
                logit = logit + bias_ref[bias_idx, h, rows, :]
            lg_scr[h, rows, :] = logit
            return logit.reshape(pk // SUBLANES, SUBLANES, tq).max(axis=0)

        def finish_max(h, cm):
            m_old = m_scr[h]
            m_new = jnp.maximum(m_old, jnp.max(cm, axis=0, keepdims=True))
            m_scr[h] = m_new
            alpha_scr[h] = jnp.exp2(m_old - m_new)

        def exp_piece(h, j):
            rows = slice(j * pk, (j + 1) * pk)
            p_scr[h, rows, :] = jnp.exp2(lg_scr[h, rows, :].reshape(pk // SUBLANES, SUBLANES, tq) - m_scr[h][None]
                                         ).reshape(pk, tq).astype(BF16)

        def value_step(h):
            pv = jnp.dot(va_ref[h, :, pl.ds(c0, tk)], p_scr[h], preferred_element_type=F32)
            acc_scr[h] = alpha_scr[h, 0:1, :] * acc_scr[h] + pv

        for h in range(n_heads):
            cm = None
            for j in range(ATT_PIECES):
                pm = logits_piece(h, j)
                cm = pm if cm is None else jnp.maximum(cm, pm)
            finish_max(h, cm)
        for h in range(n_heads):
            for j in range(ATT_PIECES):
                exp_piece(h, j)
        for h in range(n_heads):
            value_step(h)

    def far_body(c, carry):
        attend(c, None)
        return carry

    lax.fori_loop(0, jnp.maximum(qi - 1, 0), far_body, 0)

    @pl.when(qi >= 1)
    def _():
        attend(qi - 1, 1)

    attend(qi, 0)
    rows = [acc_scr[h, 0:dh, :] / acc_scr[h, dh:dh + 1, :] for h in range(n_heads)]
    o_ref[...] = jnp.concatenate(rows, axis=0).T.astype(o_ref.dtype)


def _dsa_prompt(qi_t, w_t, kib, q_t, k_bf, v_aug, bias_tab, n_sel):
    n_b, width, seq = q_t.shape
    tq = ATT_TILE
    nq = seq // tq
    va_rows = v_aug.shape[2]
    kernel = functools.partial(_dsa_prompt_kernel, n_sel)
    return pl.pallas_call(
        kernel,
        grid=(n_b, nq),
        in_specs=[
            pl.BlockSpec((None, qi_t.shape[1], tq), lambda b, q: (b, 0, q)),
            pl.BlockSpec((None, w_t.shape[1], tq), lambda b, q: (b, 0, q)),
            _single((None, seq, kib.shape[2]), lambda b, q: (b, 0, 0)),
            pl.BlockSpec((None, width, tq), lambda b, q: (b, 0, q)),
            _single((None, seq, width), lambda b, q: (b, 0, 0)),
            _single((None, N_HEADS, va_rows, seq), lambda b, q: (b, 0, 0, 0)),
            _single(bias_tab.shape, lambda b, q: (0, 0, 0, 0)),
        ],
        out_specs=pl.BlockSpec((None, tq, width), lambda b, q: (b, q, 0)),
        out_shape=jax.ShapeDtypeStruct((n_b, seq, width), BF16),
        scratch_shapes=[
            pltpu.VMEM((nq, tq, tq), I32),
            pltpu.VMEM((tq, tq), F32),
            pltpu.VMEM((N_HEADS, 2 * HEAD_DIM, tq), BF16),
            pltpu.VMEM((N_HEADS, va_rows, tq), F32),
            pltpu.VMEM((N_HEADS, SUBLANES, tq), F32),
            pltpu.VMEM((N_HEADS, SUBLANES, tq), F32),
            pltpu.VMEM((N_HEADS, tq, tq), F32),
            pltpu.VMEM((N_HEADS, tq, tq), BF16),
        ],
        compiler_params=_cparams("arbitrary", "arbitrary"),
        name="dsa_prompt",
    )(qi_t, w_t, kib, q_t, k_bf, v_aug, bias_tab)


def _sample_score_kernel(layer, n_pages, pt_ref, qi_ref, w_ref, knew_ref, kcache_ref, o_ref, kbuf, sem):
    b = pl.program_id(0)
    nb = pl.num_programs(0)
    slot = b % 2

    def page_copy(seq_id, p, s):
        return pltpu.make_async_copy(kcache_ref.at[layer, pt_ref[seq_id, p]], kbuf.at[s, p], sem.at[s])

    def start_all(seq_id, s):
        def body(p, c):
            page_copy(seq_id, p, s).start()
            return c

        lax.fori_loop(0, n_pages, body, 0)

    @pl.when(b == 0)
    def _():
        start_all(0, 0)

    @pl.when(b + 1 < nb)
    def _():
        start_all(b + 1, 1 - slot)

    def wait(p, c):
        page_copy(b, p, slot).wait()
        return c

    lax.fori_loop(0, n_pages, wait, 0)

    qi = qi_ref[0]
    w = w_ref[0]

    def score_rows(keys_t):
        d = jnp.dot(qi, keys_t.astype(BF16), preferred_element_type=F32)
        return jnp.sum(w * jnp.maximum(d, 0.0), axis=0, keepdims=True)

    group = _pick_tile(n_pages, (8, 4, 2, 1))

    def page_body(g, c):
        p0 = pl.multiple_of(g * group, group)
        rows = [score_rows(kbuf[slot, p0 + j]) for j in range(group)]
        o_ref[0, pl.ds(p0, group), :] = jnp.concatenate(rows, axis=0)
        return c

    lax.fori_loop(0, n_pages // group, page_body, 0)
    own = score_rows(knew_ref[0])
    lane = lax.broadcasted_iota(I32, own.shape, 1)
    o_ref[0, n_pages:n_pages + 1, :] = jnp.where(lane == 0, own, NEG_INF)
    n_rows = o_ref.shape[1]
    if n_rows > n_pages + 1:
        o_ref[0, n_pages + 1:, :] = jnp.full((n_rows - n_pages - 1, PAGE_SIZE), NEG_INF, F32)


def _sample_scores(layer, page_table, qi_s, w_s, kidx_new_pad, cache_kidx_t):
    n_b, n_pages = page_table.shape
    n_rows = _round_up(n_pages + 1, SUBLANES)
    grid_spec = pltpu.PrefetchScalarGridSpec(
        num_scalar_prefetch=1,
        grid=(n_b,),
        in_specs=[
            pl.BlockSpec((1, N_IDX_HEADS, IDX_DIM), lambda b, pt: (b, 0, 0)),
            pl.BlockSpec((1, N_IDX_HEADS, 1), lambda b, pt: (b, 0, 0)),
            pl.BlockSpec((1, IDX_DIM, PAGE_SIZE), lambda b, pt: (b, 0, 0)),
            pl.BlockSpec(memory_space=pl.ANY),
        ],
        out_specs=pl.BlockSpec((1, n_rows, PAGE_SIZE), lambda b, pt: (b, 0, 0)),
        scratch_shapes=[pltpu.VMEM((2, n_pages, IDX_DIM, PAGE_SIZE), F32), pltpu.SemaphoreType.DMA((2,))],
    )
    return pl.pallas_call(
        functools.partial(_sample_score_kernel, layer, n_pages),
        grid_spec=grid_spec,
        out_shape=jax.ShapeDtypeStruct((n_b, n_rows, PAGE_SIZE), F32),
        compiler_params=_cparams("arbitrary"),
        name="sample_scores",
    )(page_table, qi_s, w_s, kidx_new_pad, cache_kidx_t)


def _sample_select_kernel(n_sel, s_ref, o_ref):
    score = s_ref[...]
    n_b, n_rows, _ = score.shape
    key = _sortable_key(score)
    pos = lax.broadcasted_iota(I32, key.shape, 1) * PAGE_SIZE + lax.broadcasted_iota(I32, key.shape, 2)

    def count(pred):
        ones = jnp.where(pred, 1, 0).astype(I32)
        return jnp.sum(jnp.sum(ones, axis=1, keepdims=True), axis=2, keepdims=True)

    thr, _ = _kth_largest_key(lambda cand: count(key >= cand), n_sel, n_rows * PAGE_SIZE, (n_b, 1, 1))
    need = n_sel - count(key > thr)

    def bis_body(it, lohi):
        lo, hi = lohi
        mid = (lo + hi) // 2
        ok = count((key == thr) & (pos < mid)) >= need
        return jnp.where(ok, lo, mid), jnp.where(ok, mid, hi)

    n_it = int(math.ceil(math.log2(n_rows * PAGE_SIZE))) + 1
    _, bound = lax.fori_loop(0, n_it, bis_body,
                             (jnp.zeros((n_b, 1, 1), I32), jnp.full((n_b, 1, 1), n_rows * PAGE_SIZE, I32)))
    sel = (key > thr) | ((key == thr) & (pos < bound))
    o_ref[...] = jnp.where(sel, 0.0, NEG_INF).astype(F32)


def _sample_select(scores, n_sel):
    return pl.pallas_call(
        functools.partial(_sample_select_kernel, n_sel),
        out_shape=jax.ShapeDtypeStruct(scores.shape, F32),
        compiler_params=pltpu.CompilerParams(vmem_limit_bytes=VMEM_LIMIT),
        name="sample_select",
    )(scores)


def _sample_attend_kernel(layer, n_pages, pt_ref, q3_ref, q2_ref, mask_ref, bias_ref, knew_ref, vnew_ref,
                          kcache_ref, vcache_ref, o_ref, kbuf, vbuf, qb_scr, p_scr, acc_scr, ksem, vsem):
    b = pl.program_id(0)
    ppc = kbuf.shape[1]
    n_chunks = n_pages // ppc

    def page_copy(cache_ref, buf, sem, c, j, s):
        return pltpu.make_async_copy(cache_ref.at[layer, pt_ref[b, c * ppc + j]], buf.at[s, j], sem.at[s])

    def start(cache_ref, buf, sem, c, s):
        for j in range(ppc):
            page_copy(cache_ref, buf, sem, c, j, s).start()

    def wait(cache_ref, buf, sem, c, s):
        for j in range(ppc):
            page_copy(cache_ref, buf, sem, c, j, s).wait()

    for c0 in range(min(2, n_chunks)):
        start(kcache_ref, kbuf, ksem, c0, c0)
        start(vcache_ref, vbuf, vsem, c0, c0)
    qb_scr[...] = jnp.broadcast_to(q3_ref[0], qb_scr.shape)

    def k_body(c, carry):
        s = c % 2
        wait(kcache_ref, kbuf, ksem, c, s)
        for j in range(ppc):
            page = c * ppc + j
            logit = jnp.sum(kbuf[s, j] * qb_scr[...], axis=1)
            p_scr[page] = logit + bias_ref[page] + mask_ref[0, pl.ds(page, 1), :]

        @pl.when(c + 2 < n_chunks)
        def _():
            start(kcache_ref, kbuf, ksem, c + 2, s)

        return carry

    lax.fori_loop(0, n_chunks, k_body, 0)
    own = jnp.sum(q2_ref[0] * knew_ref[0], axis=1, keepdims=True)
    lane = lax.broadcasted_iota(I32, (N_HEADS, PAGE_SIZE), 1)
    p_scr[n_pages] = jnp.where(lane == 0, own, NEG_INF) + bias_ref[n_pages] + mask_ref[0, n_pages:n_pages + 1, :]

    logits = p_scr[...]
    m = jnp.max(jnp.max(logits, axis=0), axis=1, keepdims=True)
    p = jnp.exp(logits - m[None])
    denom = jnp.sum(jnp.sum(p, axis=0), axis=1, keepdims=True)
    p_scr[...] = p

    acc_scr[...] = jnp.zeros(acc_scr.shape, F32)

    def v_body(c, carry):
        s = c % 2
        wait(vcache_ref, vbuf, vsem, c, s)
        for h in range(N_HEADS):
            acc = acc_scr[h]
            for j in range(ppc):
                acc = acc + p_scr[c * ppc + j, h:h + 1, :] * vbuf[s, j, h]
            acc_scr[h] = acc

        @pl.when(c + 2 < n_chunks)
        def _():
            start(vcache_ref, vbuf, vsem, c + 2, s)

        return carry

    lax.fori_loop(0, n_chunks, v_body, 0)
    out = jnp.sum(acc_scr[...], axis=2) + p_scr[n_pages][:, 0:1] * vnew_ref[0]
    o_ref[0] = out / denom


def _sample_attend(layer, page_table, q_s, mask_add, bias_tab, k_new, v_new, cache_k_t, cache_v_t):
    n_b, n_pages = page_table.shape
    n_rows = mask_add.shape[1]
    ppc = _pick_tile(n_pages, (16, 8, 4, 2, 1))
    slab = (N_HEADS, HEAD_DIM, PAGE_SIZE)
    grid_spec = pltpu.PrefetchScalarGridSpec(
        num_scalar_prefetch=1,
        grid=(n_b,),
        in_specs=[
            pl.BlockSpec((1, N_HEADS, HEAD_DIM, 1), lambda b, pt: (b, 0, 0, 0)),
            pl.BlockSpec((1, N_HEADS, HEAD_DIM), lambda b, pt: (b, 0, 0)),
            pl.BlockSpec((1, n_rows, PAGE_SIZE), lambda b, pt: (b, 0, 0)),
            pl.BlockSpec(bias_tab.shape, lambda b, pt: (0, 0, 0)),
            pl.BlockSpec((1, N_HEADS, HEAD_DIM), lambda b, pt: (b, 0, 0)),
            pl.BlockSpec((1, N_HEADS, HEAD_DIM), lambda b, pt: (b, 0, 0)),
            pl.BlockSpec(memory_space=pl.ANY),
            pl.BlockSpec(memory_space=pl.ANY),
        ],
        out_specs=pl.BlockSpec((1, N_HEADS, HEAD_DIM), lambda b, pt: (b, 0, 0)),
        scratch_shapes=[pltpu.VMEM((2, ppc) + slab, F32), pltpu.VMEM((2, ppc) + slab, F32),
                        pltpu.VMEM(slab, F32), pltpu.VMEM((n_pages + 1, N_HEADS, PAGE_SIZE), F32),
                        pltpu.VMEM(slab, F32), pltpu.SemaphoreType.DMA((2,)), pltpu.SemaphoreType.DMA((2,))],
    )
    return pl.pallas_call(
        functools.partial(_sample_attend_kernel, layer, n_pages),
        grid_spec=grid_spec,
        out_shape=jax.ShapeDtypeStruct((n_b, N_HEADS, HEAD_DIM), F32),
        compiler_params=_cparams("arbitrary"),
        name="sample_attend",
    )(page_table, q_s[..., None], q_s, mask_add, bias_tab, k_new, v_new, cache_k_t, cache_v_t)


def _layer_norm(x, g, b):
    mu = jnp.mean(x, axis=-1, keepdims=True)
    xc = x - mu
    var = jnp.mean(xc * xc, axis=-1, keepdims=True)
    return xc * lax.rsqrt(var + LN_EPS) * g + b


def _split_bf16(x):
    hi = x.astype(BF16)
    lo = (x - hi.astype(F32)).astype(BF16)
    return hi, lo


def _merge_kernel(alpha, x_ref, ga_ref, gb_ref, gc_ref, ya_ref, yb_ref, yc_ref, pa_ref, pb_ref, pc_ref, wo_ref,
                  g1_ref, b1_ref, wr_hi_ref, wr_lo_ref, br_ref, x1_ref, x1t_ref, te_ref, tg_ref):
    def branch(g_ref, y_ref, p_ref):
        return jax.nn.sigmoid(g_ref[...]) * jnp.dot(y_ref[...], p_ref[...], preferred_element_type=F32)

    merged = branch(ga_ref, ya_ref, pa_ref) + branch(gb_ref, yb_ref, pb_ref) + branch(gc_ref, yc_ref, pc_ref)
    y = alpha * x_ref[...] + jnp.dot(merged.astype(BF16), wo_ref[...], preferred_element_type=F32)
    x1 = _layer_norm(y, g1_ref[...], b1_ref[...])
    x1_ref[...] = x1
    x1t_ref[...] = x1.reshape(x1t_ref.shape)

    x_hi, x_lo = _split_bf16(x1)
    dims = (((1,), (1,)), ((), ()))
    logits = lax.dot_general(wr_hi_ref[...], x_hi, dims, preferred_element_type=F32)
    logits = logits + lax.dot_general(wr_hi_ref[...], x_lo, dims, preferred_element_type=F32)
    logits = logits + lax.dot_general(wr_lo_ref[...], x_hi, dims, preferred_element_type=F32)
    logits = logits + br_ref[...]
    eidx = lax.broadcasted_iota(I32, logits.shape, 0)
    vals, idxs = [], []
    cur = logits
    for _ in range(TOP_K):
        v = jnp.max(cur, axis=0, keepdims=True)
        i = jnp.min(jnp.where(cur == v, eidx, N_EXPERTS), axis=0, keepdims=True)
        vals.append(v)
        idxs.append(i)
        cur = jnp.where(eidx == i, NEG_INF, cur)
    top_v = jnp.concatenate(vals, axis=0)
    e = jnp.exp(top_v - vals[0])
    tg_ref[...] = e / jnp.sum(e, axis=0, keepdims=True)
    te_ref[...] = jnp.concatenate(idxs, axis=0)


def _merge(alpha, x, z, y_a, y_b, y_c, p_a, p_b, p_c, w_o, ln_g, ln_b, wr_hi, wr_lo, b_r):
    t, d = x.shape
    cc = y_a.shape[1]
    tm = ROW_TILE

    def const(a):
        return pl.BlockSpec(a.shape, lambda i: (0,) * a.ndim)

    def rows(width, col=0):
        return pl.BlockSpec((tm, width), lambda i, col=col: (i, col))

    return pl.pallas_call(
        functools.partial(_merge_kernel, alpha),
        grid=(t // tm,),
        in_specs=[rows(d), rows(d, 0), rows(d, 1), rows(d, 2), rows(cc), rows(cc), rows(cc),
                  const(p_a), const(p_b), const(p_c), const(w_o), const(ln_g), const(ln_b),
                  const(wr_hi), const(wr_lo), const(b_r)],
        out_specs=[rows(d), pl.BlockSpec((tm, d // LANES, LANES), lambda i: (i, 0, 0)),
                   pl.BlockSpec((TOP_K, tm), lambda i: (0, i)), pl.BlockSpec((TOP_K, tm), lambda i: (0, i))],
        out_shape=[jax.ShapeDtypeStruct((t, d), F32), jax.ShapeDtypeStruct((t, d // LANES, LANES), F32),
                   jax.ShapeDtypeStruct((TOP_K, t), I32), jax.ShapeDtypeStruct((TOP_K, t), F32)],
        compiler_params=_cparams("parallel"),
        name="merge_ln_router",
    )(x, z, z, z, y_a, y_b, y_c, p_a, p_b, p_c, w_o, ln_g, ln_b, wr_hi, wr_lo, b_r)


class _RowGather:
    def __init__(self, idx_ref, src_ref, buf, idx_smem, isem, dsem, rows):
        self.idx_ref, self.src_ref, self.buf = idx_ref, src_ref, buf
        self.idx_smem, self.isem, self.dsem, self.rows = idx_smem, isem, dsem, rows

    def _idx_copy(self, step):
        s = step % 2
        return pltpu.make_async_copy(self.idx_ref.at[pl.ds(step * SUBLANES, SUBLANES), :], self.idx_smem.at[s],
                                     self.isem.at[s])

    def _row_copy(self, token, s, r):
        return pltpu.make_async_copy(self.src_ref.at[token], self.buf.at[s, r], self.dsem.at[s])

    def issue_rows(self, step):
        s = step % 2
        for r in range(self.rows):
            self._row_copy(self.idx_smem[s, r // LANES, r % LANES], s, r).start()
        self._idx_copy(step + 1).start()

    def begin(self, i):
        @pl.when(i == 0)
        def _():
            self._idx_copy(0).start()
            self._idx_copy(0).wait()
            self.issue_rows(0)

        self._idx_copy(i + 1).wait()

    def wait_rows(self, i):
        s = i % 2
        for r in range(self.rows):
            self._row_copy(0, s, r).wait()

    def drain(self, n):
        self.wait_rows(n)
        self._idx_copy(n + 1).wait()


def _moe_kernel(layer, blk_e_ref, n_used_ref, idx_ref, x_hbm_ref, rg_ref, wg_ref, bg_ref, wu_ref, bu_ref, wd_ref,
                bd_ref, o_ref, xbuf, wbf, idx_smem, isem, dsem):
    i = pl.program_id(0)
    n_used = n_used_ref[0]
    bm = xbuf.shape[1]
    gather = _RowGather(idx_ref, x_hbm_ref, xbuf, idx_smem, isem, dsem, bm)

    @pl.when(i == n_used)
    def _():
        gather.drain(i)

    @pl.when(i < n_used)
    def _():
        gather.begin(i)
        gather.wait_rows(i)

        @pl.when((i == 0) | (blk_e_ref[i] != blk_e_ref[jnp.maximum(i - 1, 0)]))
        def _():
            wbf[0] = wg_ref[0, 0].astype(BF16)
            wbf[1] = wu_ref[0, 0].astype(BF16)
            wbf[2] = wd_ref[0, 0].astype(BF16)

        gather.issue_rows(i + 1)
        xe = xbuf[i % 2].reshape(bm, wbf.shape[1]).astype(BF16)
        g = jnp.minimum(jnp.dot(xe, wbf[0], preferred_element_type=F32) + bg_ref[0, 0], SWIGLU_LIMIT)
        up = jnp.clip(jnp.dot(xe, wbf[1], preferred_element_type=F32) + bu_ref[0, 0], -SWIGLU_LIMIT, SWIGLU_LIMIT)
        h = g * jax.nn.sigmoid(SWIGLU_ALPHA * g) * (up + 1.0)
        y = (jnp.dot(h.astype(BF16), wbf[2], preferred_element_type=F32) + bd_ref[0, 0]) * rg_ref[...]
        o_ref[...] = y.reshape(o_ref.shape)

    @pl.when(i >= n_used)
    def _():
        o_ref[...] = jnp.zeros(o_ref.shape, o_ref.dtype)


def _moe_experts(layer, blk_e, n_used, idx_tiles, x1t, row_gate, w_gate, b_gate, w_up, b_up, w_down, b_down):
    n_blocks = blk_e.shape[0]
    bm = MOE_BLOCK
    tile = x1t.shape[1:]
    d = tile[0] * tile[1]
    ff = w_gate.shape[3]
    assert d == ff

    def wspec(shape):
        return pl.BlockSpec((1, 1) + shape, lambda i, be, nu: (layer, be[i], 0, 0))

    grid_spec = pltpu.PrefetchScalarGridSpec(
        num_scalar_prefetch=2,
        grid=(n_blocks,),
        in_specs=[pl.BlockSpec(memory_space=pl.ANY), pl.BlockSpec(memory_space=pl.ANY),
                  pl.BlockSpec((bm, 1), lambda i, be, nu: (i, 0)),
                  wspec((d, ff)), wspec((1, ff)), wspec((d, ff)), wspec((1, ff)), wspec((ff, d)), wspec((1, d))],
        out_specs=pl.BlockSpec((bm,) + tile, lambda i, be, nu: (i, 0, 0)),
        scratch_shapes=[pltpu.VMEM((2, bm) + tile, F32), pltpu.VMEM((3, d, ff), BF16),
                        pltpu.SMEM((2, SUBLANES, LANES), I32),
                        pltpu.SemaphoreType.DMA((2,)), pltpu.SemaphoreType.DMA((2,))],
    )
    return pl.pallas_call(
        functools.partial(_moe_kernel, layer),
        grid_spec=grid_spec,
        out_shape=jax.ShapeDtypeStruct((n_blocks * bm,) + tile, F32),
        compiler_params=_cparams("arbitrary"),
        name="moe_experts",
    )(blk_e, n_used, idx_tiles, x1t, row_gate, w_gate, b_gate, w_up, b_up, w_down, b_down)


def _combine_kernel(alpha, idx_ref, x_ref, y_hbm_ref, lg_ref, lb_ref, o_ref, ybuf, idx_smem, isem, dsem):
    i = pl.program_id(0)
    n = pl.num_programs(0) - 1
    tm = x_ref.shape[0]
    gather = _RowGather(idx_ref, y_hbm_ref, ybuf, idx_smem, isem, dsem, TOP_K * tm)

    @pl.when(i == n)
    def _():
        gather.drain(i)

    @pl.when(i < n)
    def _():
        gather.begin(i)
        gather.wait_rows(i)
        gather.issue_rows(i + 1)
        slot = i % 2
        h = ybuf[slot, 0:tm]
        for k in range(1, TOP_K):
            h = h + ybuf[slot, k * tm:(k + 1) * tm]
        o_ref[...] = _layer_norm(alpha * x_ref[...] + h.reshape(x_ref.shape), lg_ref[...], lb_ref[...])


def _combine(alpha, x1, ys_t, idx_tiles, ln_g, ln_b):
    t, d = x1.shape
    tm = COMBINE_TILE
    n_tiles = t // tm
    tile = ys_t.shape[1:]

    def rows(i):
        return (jnp.minimum(i, n_tiles - 1), 0)

    return pl.pallas_call(
        functools.partial(_combine_kernel, alpha),
        grid=(n_tiles + 1,),
        in_specs=[pl.BlockSpec(memory_space=pl.ANY),
                  pl.BlockSpec((tm, d), rows),
                  pl.BlockSpec(memory_space=pl.ANY),
                  pl.BlockSpec(ln_g.shape, lambda i: (0, 0)),
                  pl.BlockSpec(ln_b.shape, lambda i: (0, 0))],
        out_specs=pl.BlockSpec((tm, d), rows),
        out_shape=jax.ShapeDtypeStruct((t, d), F32),
        scratch_shapes=[pltpu.VMEM((2, TOP_K * tm) + tile, F32), pltpu.SMEM((2, SUBLANES, LANES), I32),
                        pltpu.SemaphoreType.DMA((2,)), pltpu.SemaphoreType.DMA((2,))],
        compiler_params=_cparams("arbitrary"),
        name="combine_ln",
    )(idx_tiles, x1, ys_t, ln_g, ln_b)


def _moe(alpha, layer, x1, x1t, top_e, top_g, w_gate, b_gate, w_up, b_up, w_down, b_down, ln_g, ln_b):
    t, d = x1.shape
    bm = MOE_BLOCK
    n_assign = t * TOP_K
    assert n_assign % bm == 0
    n_blocks = n_assign // bm + N_EXPERTS
    n_rows = n_blocks * bm
    flat_e = top_e.T.reshape(-1)
    experts = jnp.arange(N_EXPERTS, dtype=I32)
    counts = jnp.sum((flat_e[:, None] == experts[None, :]).astype(I32), axis=0)
    padded = (counts + bm - 1) // bm * bm
    pad_end = jnp.cumsum(padded)
    pad_key = jnp.where(jnp.arange(bm, dtype=I32)[None, :] < (padded - counts)[:, None], experts[:, None], N_EXPERTS)
    keys = jnp.concatenate([flat_e, pad_key.reshape(-1)])
    src = jnp.concatenate([jnp.arange(n_assign, dtype=I32), jnp.full((N_EXPERTS * bm,), n_assign, I32)])
    gates = jnp.concatenate([top_g.T.reshape(-1), jnp.zeros((N_EXPERTS * bm,), F32)])
    _, row_src, row_gate = lax.sort((keys, src, gates), num_keys=1, is_stable=True)
    row_token = jnp.where(row_src < n_assign, row_src // TOP_K, 0)
    _, rows_by_src = lax.sort((row_src, jnp.arange(n_rows, dtype=I32)), num_keys=1)
    pos = rows_by_src[:n_assign]
    blk_start = jnp.arange(n_blocks, dtype=I32) * bm
    blk_e = jnp.minimum(jnp.sum((pad_end[None, :] <= blk_start[:, None]).astype(I32), axis=1), N_EXPERTS - 1)
    n_used = (pad_end[-1:] // bm).astype(I32)
    row_gate = row_gate.reshape(n_rows, 1)

    per_tile = SUBLANES * LANES
    ct = COMBINE_TILE

    def tiles(a):
        return jnp.pad(a, ((0, 2), (0, per_tile - a.shape[1]))).reshape(-1, LANES)

    tok_tiles = tiles(row_token.reshape(n_blocks, bm))
    pos_tiles = tiles(pos.reshape(t // ct, ct, TOP_K).transpose(0, 2, 1).reshape(t // ct, ct * TOP_K))
    ys_t = _moe_experts(layer, blk_e, n_used, tok_tiles, x1t, row_gate, w_gate, b_gate, w_up, b_up, w_down, b_down)
    return _combine(alpha, x1, ys_t, pos_tiles, ln_g, ln_b)


def _t5_bucket(dist):
    exact = NUM_BUCKETS // 2
    d = jnp.maximum(dist, 1).astype(F32)
    large = exact + (jnp.log(d / exact) / math.log(MAX_DISTANCE / exact) * (NUM_BUCKETS - exact)).astype(I32)
    return jnp.where(dist < exact, dist, jnp.minimum(large, NUM_BUCKETS - 1))


def _toeplitz(v, n):
    length = 2 * n - 1
    hankel = jnp.tile(v, (n + 1, 1))[: n * (length + 1)].reshape(n, length + 1, v.shape[1])[:, :n, :]
    return hankel[::-1].transpose(2, 0, 1)


def _prompt_bias_tables(rel_bias, tile):
    far = rel_bias[NUM_BUCKETS - 1]
    by_dist = (rel_bias[_t5_bucket(jnp.arange(2 * tile, dtype=I32))] - far[None, :]) * LOG2_E
    v0 = jnp.concatenate([jnp.full((tile - 1, by_dist.shape[1]), NEG_INF, F32), by_dist[:tile]], axis=0)
    return jnp.stack([_toeplitz(v0, tile), _toeplitz(by_dist[1:], tile)]).astype(F32)


def kernel(x_prompt, x_sample, cache_k, cache_v, cache_kidx, state_conv, state_pool, page_table, rel_bias,
           w_in, b_in, conv_w, pool_w, pool_scale, p_a, p_b, p_c, w_o, ln1_g, ln1_b,
           w_router, b_router, w_gate, b_gate, w_up, b_up, w_down, b_down, ln2_g, ln2_b):
    n_b, seq, d = x_prompt.shape
    n_s, dec_seq, _ = x_sample.shape
    assert dec_seq == 1
    depth = w_in.shape[0]
    n_pages = page_table.shape[1]
    past = n_pages * PAGE_SIZE
    cc = conv_w.shape[2]
    pw = pool_scale.shape[1]
    aw = N_HEADS * HEAD_DIM
    iw = N_IDX_HEADS * IDX_DIM
    assert cc == pw == aw == iw and seq % ATT_TILE == 0 and d % cc == 0
    alpha = float((2 * depth) ** 0.25)
    n_sel_p = min(TOPK_MAX, seq // 4)
    n_sel_s = min(TOPK_MAX, (past + dec_seq) // 4)

    t_p = n_b * seq
    t_all = _round_up(t_p + n_s, ROW_TILE)
    x = jnp.concatenate([x_prompt.reshape(t_p, d), x_sample.reshape(n_s, d),
                         jnp.zeros((t_all - t_p - n_s, d), F32)], axis=0)

    sizes = (cc, cc, cc, pw, aw, aw, aw, iw, IDX_DIM, N_IDX_HEADS, 3 * d)
    n_in = sum(sizes)
    off_g = n_in - 3 * d
    nz = _round_up(n_in, 1536)
    c_ab = 3 * d
    c_q = c_ab + 4 * cc
    c_ki = c_q + 4 * cc
    bias_tab = _prompt_bias_tables(rel_bias, ATT_TILE)
    pt = page_table.astype(I32)
    key_pos = jnp.arange((n_pages + 1) * PAGE_SIZE, dtype=I32)
    bias_tab_s = rel_bias[_t5_bucket(jnp.clip(past - key_pos, 0, MAX_DISTANCE))]
    bias_tab_s = bias_tab_s.reshape(n_pages + 1, PAGE_SIZE, N_HEADS).transpose(0, 2, 1)
    cache_kidx_t = cache_kidx.transpose(0, 1, 3, 2)
    cache_k_t = cache_k.transpose(0, 1, 3, 4, 2)
    cache_v_t = cache_v.transpose(0, 1, 3, 4, 2)

    outs = {name: [] for name in ("kp", "vp", "kip", "cp", "pp", "ks", "vs", "kis", "cs", "ps")}
    for l in range(depth):
        w_l = jnp.concatenate([w_in[l][:, off_g:], w_in[l][:, :off_g], jnp.zeros((d, nz - n_in), F32)], axis=1)
        b_l = jnp.concatenate([b_in[l][off_g:], b_in[l][:off_g], jnp.zeros((nz - n_in,), F32)])[None, :]
        z = _inproj(x, w_l.astype(BF16), b_l)

        def col(c0, width, rows=slice(None)):
            return z[rows, c0:c0 + width]

        pr = slice(0, t_p)
        sr = slice(t_p, t_p + n_s)
        conv_w_l = conv_w[l]
        pool_w_l = pool_w[l].astype(BF16)
        pool_s_l = pool_scale[l][None, :]

        ya_p, yb_p = _mixer_prompt(z, c_ab, n_b, seq, cc, conv_w_l, pool_w_l, pool_s_l)

        q_p = col(c_q, aw, pr).reshape(n_b, seq, aw)
        k_p = col(c_q + aw, aw, pr).reshape(n_b, seq, aw)
        v_p = col(c_q + 2 * aw, aw, pr).reshape(n_b, seq, aw)
        qi_p = col(c_q + 3 * aw, iw, pr).reshape(n_b, seq, iw)
        ki_p = col(c_ki, IDX_DIM, pr).reshape(n_b, seq, IDX_DIM)
        wi_p = col(c_ki + IDX_DIM, N_IDX_HEADS, pr).reshape(n_b, seq, N_IDX_HEADS)
        qi_t = (qi_p * (IDX_DIM ** -0.5)).astype(BF16).transpose(0, 2, 1)
        w_t = (wi_p * (N_IDX_HEADS ** -0.5)).transpose(0, 2, 1)
        q_t = (q_p * (HEAD_DIM ** -0.5 * LOG2_E)).astype(BF16).transpose(0, 2, 1)
        v_heads_t = v_p.astype(BF16).reshape(n_b, seq, N_HEADS, HEAD_DIM).transpose(0, 2, 3, 1)
        v_aug = jnp.concatenate([v_heads_t, jnp.ones((n_b, N_HEADS, VA_ONES, seq), BF16)], axis=2)
        yc_p = _dsa_prompt(qi_t, w_t, ki_p.astype(BF16), q_t, k_p.astype(BF16), v_aug, bias_tab, n_sel_p)

        ya_s, yb_s, u_s = _mixer_sample(
            col(c_ab, cc, sr), col(c_ab + cc, cc, sr), col(c_ab + 2 * cc, cc, sr), col(c_ab + 3 * cc, pw, sr),
            state_conv[l].transpose(1, 0, 2), state_pool[l].transpose(1, 0, 2), past, conv_w_l, pool_w_l, pool_s_l)

        q_s = col(c_q, aw, sr).reshape(n_s, N_HEADS, HEAD_DIM)
        k_s = col(c_q + aw, aw, sr).reshape(n_s, N_HEADS, HEAD_DIM)
        v_s = col(c_q + 2 * aw, aw, sr).reshape(n_s, N_HEADS, HEAD_DIM)
        qi_s = (col(c_q + 3 * aw, iw, sr) * (IDX_DIM ** -0.5)).astype(BF16).reshape(n_s, N_IDX_HEADS, IDX_DIM)
        ki_s = col(c_ki, IDX_DIM, sr)
        wi_s = (col(c_ki + IDX_DIM, N_IDX_HEADS, sr) * (N_IDX_HEADS ** -0.5))[:, :, None]
        ki_new_pad = jnp.pad(ki_s[:, :, None], ((0, 0), (0, 0), (0, PAGE_SIZE - 1)))
        scores = _sample_scores(l, pt, qi_s, wi_s, ki_new_pad, cache_kidx_t)
        mask_add = _sample_select(scores, n_sel_s)
        yc_s = _sample_attend(l, pt, q_s * (HEAD_DIM ** -0.5), mask_add, bias_tab_s, k_s, v_s,
                              cache_k_t, cache_v_t).reshape(n_s, aw)

        pad_rows = t_all - t_p - n_s

        def all_rows(yp, ys_):
            return jnp.concatenate([yp, ys_, jnp.zeros((pad_rows, yp.shape[1]), BF16)], axis=0)

        wr_hi, wr_lo = _split_bf16(w_router[l].T)
        x1, x1t, top_e, top_g = _merge(
            alpha, x, z, all_rows(ya_p, ya_s), all_rows(yb_p, yb_s),
            all_rows(yc_p.reshape(t_p, aw), yc_s.astype(BF16)),
            p_a[l].astype(BF16), p_b[l].astype(BF16), p_c[l].astype(BF16), w_o[l].astype(BF16),
            ln1_g[l][None, :], ln1_b[l][None, :], wr_hi, wr_lo, b_router[l][:, None])

        x = _moe(alpha, l, x1, x1t, top_e, top_g, w_gate, b_gate[:, :, None, :], w_up, b_up[:, :, None, :],
                 w_down, b_down[:, :, None, :], ln2_g[l][None, :], ln2_b[l][None, :])

        u_p = (col(c_ab + cc, cc, pr) * col(c_ab + 2 * cc, cc, pr)).reshape(n_b, seq, cc)
        up_p = col(c_ab + 3 * cc, pw, pr).reshape(n_b, seq, pw)
        outs["kp"].append(k_p.reshape(n_b, seq, N_HEADS, HEAD_DIM))
        outs["vp"].append(v_p.reshape(n_b, seq, N_HEADS, HEAD_DIM))
        outs["kip"].append(ki_p)
        outs["cp"].append(u_p[:, seq - (CONV_WIDTH - 1):])
        outs["pp"].append(up_p[:, seq - POOL_STATE:])
        outs["ks"].append(k_s[:, None])
        outs["vs"].append(v_s[:, None])
        outs["kis"].append(ki_s[:, None])
        outs["cs"].append(jnp.concatenate([state_conv[l], u_s[:, None]], axis=1)[:, -(CONV_WIDTH - 1):])
        outs["ps"].append(jnp.concatenate([state_pool[l], col(c_ab + 3 * cc, pw, sr)[:, None]], axis=1)[:, -POOL_STATE:])

    y_prompt = x[:t_p].reshape(n_b, seq, d)
    y_sample = x[t_p:t_p + n_s].reshape(n_s, dec_seq, d)
    stk = {k: jnp.stack(v) for k, v in outs.items()}
    return (y_prompt, y_sample, stk["kp"], stk["vp"], stk["kip"], stk["cp"], stk["pp"],
            stk["ks"], stk["vs"], stk["kis"], stk["cs"], stk["ps"])
```

```python
import functools
import math

import numpy as np
import jax
import jax.numpy as jnp
from jax import lax
from jax.experimental import pallas as pl
from jax.experimental.pallas import tpu as pltpu

F32 = jnp.float32
BF16 = jnp.bfloat16
I32 = jnp.int32

N_HEADS = 8
HEAD_DIM = 64
N_IDX_HEADS = 8
IDX_DIM = 64
TOPK_MAX = 256
POOL_WINDOWS = (2, 4, 8, 16)
POOL_STATE = max(POOL_WINDOWS) - 1
CONV_WIDTH = 3
N_EXPERTS = 32
TOP_K = 4
SWIGLU_LIMIT = 7.0
SWIGLU_ALPHA = 1.702
LN_EPS = 1e-5
NUM_BUCKETS = 32
MAX_DISTANCE = 128
PAGE_SIZE = 128

LANES = 128
SUBLANES = 8
VMEM_LIMIT = 56 * 1024 * 1024

ROW_TILE = 256
ATT_TILE = 256
ATT_PIECES = 2
COUNT_UNROLL = 4
VA_ONES = 16
MOE_BLOCK = 256
HALO = 16
INT_MIN = np.int32(-2 ** 31)
LOG2_E = math.log2(math.e)
NEG_INF = float("-inf")


def _cparams(*sem):
    return pltpu.CompilerParams(dimension_semantics=sem, vmem_limit_bytes=VMEM_LIMIT)


def _round_up(n, m):
    return (n + m - 1) // m * m


def _pick_tile(n, candidates):
    for c in candidates:
        if n % c == 0:
            return c
    raise ValueError(f"no tile for {n} in {candidates}")


def _single(block_shape, index_map):
    return pl.BlockSpec(block_shape, index_map, pipeline_mode=pl.Buffered(1))


def _inproj_kernel(x_ref, w_ref, b_ref, o_ref):
    x = x_ref[...].astype(BF16)
    o_ref[...] = jnp.dot(x, w_ref[...], preferred_element_type=F32) + b_ref[...]


def _inproj(x, w, b):
    m, k = x.shape
    n = w.shape[1]
    tm = _pick_tile(m, (1280, 1024, 768, 512, 256))
    tn = _pick_tile(n, (1536, 1280, 1024, 768, 512, 256, 128))
    return pl.pallas_call(
        _inproj_kernel,
        grid=(n // tn, m // tm),
        in_specs=[
            pl.BlockSpec((tm, k), lambda j, i: (i, 0)),
            pl.BlockSpec((k, tn), lambda j, i: (0, j)),
            pl.BlockSpec((1, tn), lambda j, i: (0, j)),
        ],
        out_specs=pl.BlockSpec((tm, tn), lambda j, i: (i, j)),
        out_shape=jax.ShapeDtypeStruct((m, n), F32),
        compiler_params=_cparams("parallel", "parallel"),
        name="inproj",
    )(x, w, b)


def _inproj_t_kernel(scale_q, scale_qi, scale_w, x_ref, w_ref, b_ref, q_ref, k_ref, v_ref, va_ref, qi_ref, ki_ref,
                     wi_ref):
    x = x_ref[...].astype(BF16)
    aw = q_ref.shape[0]
    dims = (((1,), (1,)), ((), ()))

    def proj(r0, n):
        return lax.dot_general(w_ref[r0:r0 + n, :], x, dims, preferred_element_type=F32) + b_ref[r0:r0 + n, :]

    q_ref[...] = (proj(0, aw) * scale_q).astype(BF16)
    k_ref[...] = proj(aw, aw)
    v = proj(2 * aw, aw)
    v_ref[...] = v
    ones = jnp.ones((va_ref.shape[1] - HEAD_DIM, va_ref.shape[2]), BF16)
    for h in range(N_HEADS):
        va_ref[h, 0:HEAD_DIM, :] = v[h * HEAD_DIM:(h + 1) * HEAD_DIM, :].astype(BF16)
        va_ref[h, HEAD_DIM:, :] = ones
    qi_ref[...] = (proj(3 * aw, aw) * scale_qi).astype(BF16)
    tail = proj(4 * aw, w_ref.shape[0] - 4 * aw)
    ki_ref[...] = tail[0:IDX_DIM, :]
    wi_ref[...] = tail[IDX_DIM:IDX_DIM + N_IDX_HEADS, :] * scale_w


def _inproj_t(x, w_t, b_t, n_tok, scale_q, scale_qi, scale_w):
    k = x.shape[1]
    aw = N_HEADS * HEAD_DIM
    tm = _pick_tile(n_tok, (1024, 512, 256))
    va_rows = HEAD_DIM + VA_ONES

    def cols(rows):
        return pl.BlockSpec((rows, tm), lambda i: (0, i))

    return pl.pallas_call(
        functools.partial(_inproj_t_kernel, scale_q, scale_qi, scale_w),
        grid=(n_tok // tm,),
        in_specs=[pl.BlockSpec((tm, k), lambda i: (i, 0)),
                  pl.BlockSpec(w_t.shape, lambda i: (0, 0)),
                  pl.BlockSpec(b_t.shape, lambda i: (0, 0))],
        out_specs=[cols(aw), cols(aw), cols(aw), pl.BlockSpec((N_HEADS, va_rows, tm), lambda i: (0, 0, i)),
                   cols(aw), cols(IDX_DIM), cols(N_IDX_HEADS)],
        out_shape=[jax.ShapeDtypeStruct((aw, n_tok), BF16), jax.ShapeDtypeStruct((aw, n_tok), F32),
                   jax.ShapeDtypeStruct((aw, n_tok), F32), jax.ShapeDtypeStruct((N_HEADS, va_rows, n_tok), BF16),
                   jax.ShapeDtypeStruct((aw, n_tok), BF16), jax.ShapeDtypeStruct((IDX_DIM, n_tok), F32),
                   jax.ShapeDtypeStruct((N_IDX_HEADS, n_tok), F32)],
        compiler_params=_cparams("parallel"),
        name="inproj_t",
    )(x, w_t, b_t)


def _pool_out(diffs, pw_ref, ps_ref):
    outs = [jnp.dot(d.astype(BF16), pw_ref[g], preferred_element_type=F32) for g, d in enumerate(diffs)]
    return jnp.concatenate(outs, axis=-1) * ps_ref[...]


def _mixer_prompt_kernel(ab_ref, ac_ref, ah_ref, up_ref, cw_ref, pw_ref, ps_ref, ya_ref, yb_ref, eu_ref, ep_ref):
    si = pl.program_id(1)
    ts = ab_ref.shape[0]
    pg = up_ref.shape[1] // len(POOL_WINDOWS)

    @pl.when(si == 0)
    def _():
        eu_ref[0:HALO, :] = jnp.zeros((HALO, eu_ref.shape[1]), F32)
        ep_ref[0:HALO, :] = jnp.zeros((HALO, ep_ref.shape[1]), F32)

    @pl.when(si > 0)
    def _():
        eu_ref[0:HALO, :] = eu_ref[ts:ts + HALO, :]
        ep_ref[0:HALO, :] = ep_ref[ts:ts + HALO, :]

    u = ac_ref[...] * ah_ref[...]
    eu_ref[HALO:HALO + ts, :] = u
    ep_ref[HALO:HALO + ts, :] = up_ref[...]
    conv = cw_ref[0:1, :] * eu_ref[HALO - 2:HALO - 2 + ts, :]
    conv = conv + cw_ref[1:2, :] * eu_ref[HALO - 1:HALO - 1 + ts, :]
    conv = conv + cw_ref[2:3, :] * u
    ya_ref[...] = (ab_ref[...] * conv).astype(ya_ref.dtype)

    pos = si * ts + lax.broadcasted_iota(I32, (ts, 1), 0)
    diffs = []
    for g, w in enumerate(POOL_WINDOWS):
        sl = slice(g * pg, (g + 1) * pg)
        win = ep_ref[HALO:HALO + ts, sl]
        for j in range(1, w):
            win = win + ep_ref[HALO - j:HALO - j + ts, sl]
        cnt = jnp.minimum(w, pos + 1).astype(F32)
        diffs.append(win / cnt - up_ref[:, sl])
    yb_ref[...] = _pool_out(diffs, pw_ref, ps_ref).astype(yb_ref.dtype)


def _mixer_prompt(z, col0, n_b, seq, cc, conv_w, pool_w_bf, pool_scale):
    ts = _pick_tile(seq, (512, 256, 128))
    nblk = seq // ts
    cb = col0 // cc

    def zspec(i):
        return pl.BlockSpec((ts, cc), lambda b, s, i=i: (b * nblk + s, cb + i))

    out_spec = pl.BlockSpec((ts, cc), lambda b, s: (b * nblk + s, 0))
    return pl.pallas_call(
        _mixer_prompt_kernel,
        grid=(n_b, nblk),
        in_specs=[zspec(0), zspec(1), zspec(2), zspec(3),
                  pl.BlockSpec(conv_w.shape, lambda b, s: (0, 0)),
                  pl.BlockSpec(pool_w_bf.shape, lambda b, s: (0, 0, 0)),
                  pl.BlockSpec(pool_scale.shape, lambda b, s: (0, 0))],
        out_specs=[out_spec, out_spec],
        out_shape=[jax.ShapeDtypeStruct((n_b * seq, cc), BF16)] * 2,
        scratch_shapes=[pltpu.VMEM((HALO + ts, cc), F32), pltpu.VMEM((HALO + ts, cc), F32)],
        compiler_params=_cparams("arbitrary", "arbitrary"),
        name="mixer_prompt",
    )(z, z, z, z, conv_w, pool_w_bf, pool_scale)


def _mixer_sample_kernel(pos0, ab_ref, ac_ref, ah_ref, up_ref, cprev_ref, pprev_ref, cw_ref, pw_ref, ps_ref,
                         ya_ref, yb_ref, u_ref):
    pg = up_ref.shape[1] // len(POOL_WINDOWS)
    u = ac_ref[...] * ah_ref[...]
    u_ref[...] = u
    conv = cw_ref[0:1, :] * cprev_ref[0] + cw_ref[1:2, :] * cprev_ref[1] + cw_ref[2:3, :] * u
    ya_ref[...] = (ab_ref[...] * conv).astype(ya_ref.dtype)
    diffs = []
    for g, w in enumerate(POOL_WINDOWS):
        sl = slice(g * pg, (g + 1) * pg)
        win = up_ref[:, sl]
        for j in range(1, w):
            win = win + pprev_ref[POOL_STATE - j, :, sl]
        cnt = float(min(w, pos0 + 1))
        diffs.append(win / cnt - up_ref[:, sl])
    yb_ref[...] = _pool_out(diffs, pw_ref, ps_ref).astype(yb_ref.dtype)


def _mixer_sample(a_b, a_c, a_h, u_pool, conv_prev_t, pool_prev_t, pos0, conv_w, pool_w_bf, pool_scale):
    n, cc = a_b.shape
    return pl.pallas_call(
        functools.partial(_mixer_sample_kernel, pos0),
        out_shape=[jax.ShapeDtypeStruct((n, cc), BF16), jax.ShapeDtypeStruct((n, cc), BF16),
                   jax.ShapeDtypeStruct((n, cc), F32)],
        compiler_params=pltpu.CompilerParams(vmem_limit_bytes=VMEM_LIMIT),
        name="mixer_sample",
    )(a_b, a_c, a_h, u_pool, conv_prev_t, pool_prev_t, conv_w, pool_w_bf, pool_scale)


def _sortable_key(score):
    bits = lax.bitcast_convert_type(score, I32)
    return bits ^ ((bits >> 31) & np.int32(0x7FFFFFFF))


def _kth_largest_key(count_ge, k, n_total, shape):
    def cond(st):
        it, _, cnt_cur = st
        return (it < 32) & jnp.any(cnt_cur != k)

    def body(st):
        it, thr_u, cnt_cur = st
        cand_u = thr_u | jnp.left_shift(jnp.int32(1), 31 - it)
        cnt = count_ge(cand_u ^ INT_MIN)
        ok = cnt >= k
        return it + 1, jnp.where(ok, cand_u, thr_u), jnp.where(ok, cnt, cnt_cur)

    init = (jnp.int32(0), jnp.zeros(shape, I32), jnp.zeros(shape, I32) + n_total)
    _, thr_u, cnt_cur = lax.while_loop(cond, body, init)
    return thr_u ^ INT_MIN, cnt_cur


def _dsa_prompt_kernel(n_sel, qi_t_ref, w_t_ref, kib_ref, q_t_ref, k_ref, va_ref, bias_ref, o_ref,
                       key_scr, mask_scr, qz_scr, acc_scr, m_scr, alpha_scr, lg_scr, p_scr):
    qi = pl.program_id(1)
    tq = o_ref.shape[0]
    tk = tq
    n_heads = N_HEADS
    dh = HEAD_DIM

    def score_chunk(c):
        r0 = pl.multiple_of(c * tk, tk)
        kc = kib_ref[pl.ds(r0, tk), :]
        acc = jnp.zeros((tk, tq), F32)
        for h in range(N_IDX_HEADS):
            d = jnp.dot(kc, qi_t_ref[h * IDX_DIM:(h + 1) * IDX_DIM, :], preferred_element_type=F32)
            acc = acc + w_t_ref[h:h + 1, :] * jnp.maximum(d, 0.0)
        return acc

    def score_body(c, carry):
        key_scr[c] = _sortable_key(score_chunk(c))
        return carry

    lax.fori_loop(0, qi, score_body, 0)
    key_pos = lax.broadcasted_iota(I32, (tk, tq), 0)
    qry_pos = lax.broadcasted_iota(I32, (tk, tq), 1)
    diag = jnp.where(key_pos <= qry_pos, score_chunk(qi), NEG_INF)
    key_scr[qi] = _sortable_key(diag)

    def count_where(pred):
        def chunk_count(c):
            ind = jnp.where(pred(key_scr[c], c), 1, 0).astype(I32)
            return ind.reshape(tk // SUBLANES, SUBLANES, tq).sum(axis=0)

        def group_body(g, accs):
            return tuple(acc + chunk_count(COUNT_UNROLL * g + u) for u, acc in enumerate(accs))

        n_groups = (qi + 1) // COUNT_UNROLL
        zero = jnp.zeros((SUBLANES, tq), I32)
        accs = lax.fori_loop(0, n_groups, group_body, (zero,) * COUNT_UNROLL)
        acc = functools.reduce(lambda a, b: a + b, accs)
        acc = lax.fori_loop(n_groups * COUNT_UNROLL, qi + 1, lambda c, a: a + chunk_count(c), acc)
        return acc.sum(axis=0, keepdims=True)

    thr, cnt_ge = _kth_largest_key(lambda cand: count_where(lambda blk, c: blk >= cand), n_sel,
                                   (qi + 1) * tk, (1, tq))

    neg_inf_key = _sortable_key(jnp.full((1, tq), NEG_INF, F32))
    overflow = (cnt_ge > n_sel) & (thr > neg_inf_key)

    @pl.when(jnp.any(overflow))
    def _():
        need = n_sel - count_where(lambda blk, c: blk > thr)

        def count_tied_below(bound):
            return count_where(lambda blk, c: (blk == thr) & (c * tk + key_pos < bound))

        def bis_body(it, lohi):
            lo, hi = lohi
            mid = (lo + hi) // 2
            ok = count_tied_below(mid) >= need
            return jnp.where(ok, lo, mid), jnp.where(ok, mid, hi)

        n_keys = (qi + 1) * tk
        lo0 = jnp.zeros((1, tq), I32)
        hi0 = jnp.zeros((1, tq), I32) + n_keys
        n_it = int(math.ceil(math.log2(key_scr.shape[0] * tk))) + 1
        _, bound = lax.fori_loop(0, n_it, bis_body, (lo0, hi0))

        def demote(c, carry):
            blk = key_scr[c]
            drop = overflow & (blk == thr) & (c * tk + key_pos >= bound)
            key_scr[c] = jnp.where(drop, thr - 1, blk)
            return carry

        lax.fori_loop(0, qi + 1, demote, 0)

    zero_half = jnp.zeros((dh, tq), BF16)
    for h in range(n_heads):
        qh = q_t_ref[h * dh:(h + 1) * dh, :]
        pair = (qh, zero_half) if h % 2 == 0 else (zero_half, qh)
        qz_scr[h] = jnp.concatenate(pair, axis=0)
    pk = tk // ATT_PIECES
    m_scr[...] = jnp.full(m_scr.shape, -1e30, F32)
    acc_scr[...] = jnp.zeros(acc_scr.shape, F32)

    def attend(c, bias_idx):
        c0 = pl.multiple_of(c * tk, tk)
        mask_scr[...] = jnp.where(key_scr[c] >= thr, 0.0, NEG_INF).astype(F32)

        def logits_piece(h, j):
            p2 = h // 2
            rows = slice(j * pk, (j + 1) * pk)
            kc = k_ref[pl.ds(pl.multiple_of(c0 + j * pk, pk), pk), p2 * 2 * dh:(p2 + 1) * 2 * dh]
            logit = jnp.dot(kc, qz_scr[h], preferred_element_type=F32) + mask_scr[rows, :]
            if bias_idx is not None:
                logit = logit + bias_ref[bias_idx, h, rows, :]
            lg_scr[h, rows, :] = logit
            return logit.reshape(pk // SUBLANES, SUBLANES, tq).max(axis=0)

        def finish_max(h, cm):
            m_old = m_scr[h]
            m_new = jnp.maximum(m_old, jnp.max(cm, axis=0, keepdims=True))
            m_scr[h] = m_new
            alpha_scr[h] = jnp.exp2(m_old - m_new)

        def exp_piece(h, j):
            rows = slice(j * pk, (j + 1) * pk)
            p_scr[h, rows, :] = jnp.exp2(lg_scr[h, rows, :].reshape(pk // SUBLANES, SUBLANES, tq) - m_scr[h][None]
                                         ).reshape(pk, tq).astype(BF16)

        def value_step(h):
            pv = jnp.dot(va_ref[h, :, pl.ds(c0, tk)], p_scr[h], preferred_element_type=F32)
            acc_scr[h] = alpha_scr[h, 0:1, :] * acc_scr[h] + pv

        for h in range(n_heads):
            cm = None
            for j in range(ATT_PIECES):
                pm = logits_piece(h, j)
                cm = pm if cm is None else jnp.maximum(cm, pm)
            finish_max(h, cm)
        for h in range(n_heads):
            for j in range(ATT_PIECES):
                exp_piece(h, j)
        for h in range(n_heads):
            value_step(h)

    def far_body(c, carry):
        attend(c, None)
        return carry

    lax.fori_loop(0, jnp.maximum(qi - 1, 0), far_body, 0)

    @pl.when(qi >= 1)
    def _():
        attend(qi - 1, 1)

    attend(qi, 0)
    rows = [acc_scr[h, 0:dh, :] / acc_scr[h, dh:dh + 1, :] for h in range(n_heads)]
    o_ref[...] = jnp.concatenate(rows, axis=0).T.astype(o_ref.dtype)


def _dsa_prompt(qi_t, w_t, kib, q_t, k_bf, v_aug, bias_tab, n_sel):
    n_b, seq, width = k_bf.shape
    tq = ATT_TILE
    nq = seq // tq
    va_rows = v_aug.shape[1]
    kernel = functools.partial(_dsa_prompt_kernel, n_sel)
    return pl.pallas_call(
        kernel,
        grid=(n_b, nq),
        in_specs=[
            pl.BlockSpec((qi_t.shape[0], tq), lambda b, q: (0, b * nq + q)),
            pl.BlockSpec((w_t.shape[0], tq), lambda b, q: (0, b * nq + q)),
            _single((None, seq, kib.shape[2]), lambda b, q: (b, 0, 0)),
            pl.BlockSpec((width, tq), lambda b, q: (0, b * nq + q)),
            _single((None, seq, width), lambda b, q: (b, 0, 0)),
            _single((N_HEADS, va_rows, seq), lambda b, q: (0, 0, b)),
            _single(bias_tab.shape, lambda b, q: (0, 0, 0, 0)),
        ],
        out_specs=pl.BlockSpec((None, tq, width), lambda b, q: (b, q, 0)),
        out_shape=jax.ShapeDtypeStruct((n_b, seq, width), BF16),
        scratch_shapes=[
            pltpu.VMEM((nq, tq, tq), I32),
            pltpu.VMEM((tq, tq), F32),
            pltpu.VMEM((N_HEADS, 2 * HEAD_DIM, tq), BF16),
            pltpu.VMEM((N_HEADS, va_rows, tq), F32),
            pltpu.VMEM((N_HEADS, SUBLANES, tq), F32),
            pltpu.VMEM((N_HEADS, SUBLANES, tq), F32),
            pltpu.VMEM((N_HEADS, tq, tq), F32),
            pltpu.VMEM((N_HEADS, tq, tq), BF16),
        ],
        compiler_params=_cparams("arbitrary", "arbitrary"),
        name="dsa_prompt",
    )(qi_t, w_t, kib, q_t, k_bf, v_aug, bias_tab)


def _sample_score_kernel(layer, n_pages, pt_ref, qi_ref, w_ref, knew_ref, kcache_ref, o_ref, kbuf, sem):
    b = pl.program_id(0)
    nb = pl.num_programs(0)
    slot = b % 2

    def page_copy(seq_id, p, s):
        return pltpu.make_async_copy(kcache_ref.at[layer, pt_ref[seq_id, p]], kbuf.at[s, p], sem.at[s])

    def start_all(seq_id, s):
        def body(p, c):
            page_copy(seq_id, p, s).start()
            return c

        lax.fori_loop(0, n_pages, body, 0)

    @pl.when(b == 0)
    def _():
        start_all(0, 0)

    @pl.when(b + 1 < nb)
    def _():
        start_all(b + 1, 1 - slot)

    def wait(p, c):
        page_copy(b, p, slot).wait()
        return c

    lax.fori_loop(0, n_pages, wait, 0)

    qi = qi_ref[0]
    w = w_ref[0]

    def score_rows(keys_t):
        d = jnp.dot(qi, keys_t.astype(BF16), preferred_element_type=F32)
        return jnp.sum(w * jnp.maximum(d, 0.0), axis=0, keepdims=True)

    group = _pick_tile(n_pages, (8, 4, 2, 1))

    def page_body(g, c):
        p0 = pl.multiple_of(g * group, group)
        rows = [score_rows(kbuf[slot, p0 + j]) for j in range(group)]
        o_ref[0, pl.ds(p0, group), :] = jnp.concatenate(rows, axis=0)
        return c

    lax.fori_loop(0, n_pages // group, page_body, 0)
    own = score_rows(knew_ref[0])
    lane = lax.broadcasted_iota(I32, own.shape, 1)
    o_ref[0, n_pages:n_pages + 1, :] = jnp.where(lane == 0, own, NEG_INF)
    n_rows = o_ref.shape[1]
    if n_rows > n_pages + 1:
        o_ref[0, n_pages + 1:, :] = jnp.full((n_rows - n_pages - 1, PAGE_SIZE), NEG_INF, F32)


def _sample_scores(layer, page_table, qi_s, w_s, kidx_new_pad, cache_kidx_t):
    n_b, n_pages = page_table.shape
    n_rows = _round_up(n_pages + 1, SUBLANES)
    grid_spec = pltpu.PrefetchScalarGridSpec(
        num_scalar_prefetch=1,
        grid=(n_b,),
        in_specs=[
            pl.BlockSpec((1, N_IDX_HEADS, IDX_DIM), lambda b, pt: (b, 0, 0)),
            pl.BlockSpec((1, N_IDX_HEADS, 1), lambda b, pt: (b, 0, 0)),
            pl.BlockSpec((1, IDX_DIM, PAGE_SIZE), lambda b, pt: (b, 0, 0)),
            pl.BlockSpec(memory_space=pl.ANY),
        ],
        out_specs=pl.BlockSpec((1, n_rows, PAGE_SIZE), lambda b, pt: (b, 0, 0)),
        scratch_shapes=[pltpu.VMEM((2, n_pages, IDX_DIM, PAGE_SIZE), F32), pltpu.SemaphoreType.DMA((2,))],
    )
    return pl.pallas_call(
        functools.partial(_sample_score_kernel, layer, n_pages),
        grid_spec=grid_spec,
        out_shape=jax.ShapeDtypeStruct((n_b, n_rows, PAGE_SIZE), F32),
        compiler_params=_cparams("arbitrary"),
        name="sample_scores",
    )(page_table, qi_s, w_s, kidx_new_pad, cache_kidx_t)


def _sample_select_kernel(n_sel, s_ref, o_ref):
    score = s_ref[...]
    n_b, n_rows, _ = score.shape
    key = _sortable_key(score)
    pos = lax.broadcasted_iota(I32, key.shape, 1) * PAGE_SIZE + lax.broadcasted_iota(I32, key.shape, 2)

    def count(pred):
        ones = jnp.where(pred, 1, 0).astype(I32)
        return jnp.sum(jnp.sum(ones, axis=1, keepdims=True), axis=2, keepdims=True)

    thr, _ = _kth_largest_key(lambda cand: count(key >= cand), n_sel, n_rows * PAGE_SIZE, (n_b, 1, 1))
    need = n_sel - count(key > thr)

    def bis_body(it, lohi):
        lo, hi = lohi
        mid = (lo + hi) // 2
        ok = count((key == thr) & (pos < mid)) >= need
        return jnp.where(ok, lo, mid), jnp.where(ok, mid, hi)

    n_it = int(math.ceil(math.log2(n_rows * PAGE_SIZE))) + 1
    _, bound = lax.fori_loop(0, n_it, bis_body,
                             (jnp.zeros((n_b, 1, 1), I32), jnp.full((n_b, 1, 1), n_rows * PAGE_SIZE, I32)))
    sel = (key > thr) | ((key == thr) & (pos < bound))
    o_ref[...] = jnp.where(sel, 0.0, NEG_INF).astype(F32)


def _sample_select(scores, n_sel):
    return pl.pallas_call(
        functools.partial(_sample_select_kernel, n_sel),
        out_shape=jax.ShapeDtypeStruct(scores.shape, F32),
        compiler_params=pltpu.CompilerParams(vmem_limit_bytes=VMEM_LIMIT),
        name="sample_select",
    )(scores)


def _sample_attend_kernel(layer, n_pages, pt_ref, q3_ref, q2_ref, mask_ref, bias_ref, knew_ref, vnew_ref,
                          kcache_ref, vcache_ref, o_ref, kbuf, vbuf, qb_scr, p_scr, acc_scr, ksem, vsem):
    b = pl.program_id(0)
    ppc = kbuf.shape[1]
    n_chunks = n_pages // ppc

    def page_copy(cache_ref, buf, sem, c, j, s):
        return pltpu.make_async_copy(cache_ref.at[layer, pt_ref[b, c * ppc + j]], buf.at[s, j], sem.at[s])

    def start(cache_ref, buf, sem, c, s):
        for j in range(ppc):
            page_copy(cache_ref, buf, sem, c, j, s).start()

    def wait(cache_ref, buf, sem, c, s):
        for j in range(ppc):
            page_copy(cache_ref, buf, sem, c, j, s).wait()

    for c0 in range(min(2, n_chunks)):
        start(kcache_ref, kbuf, ksem, c0, c0)
        start(vcache_ref, vbuf, vsem, c0, c0)
    qb_scr[...] = jnp.broadcast_to(q3_ref[0], qb_scr.shape)

    def k_body(c, carry):
        s = c % 2
        wait(kcache_ref, kbuf, ksem, c, s)
        for j in range(ppc):
            page = c * ppc + j
            logit = jnp.sum(kbuf[s, j] * qb_scr[...], axis=1)
            p_scr[page] = logit + bias_ref[page] + mask_ref[0, pl.ds(page, 1), :]

        @pl.when(c + 2 < n_chunks)
        def _():
            start(kcache_ref, kbuf, ksem, c + 2, s)

        return carry

    lax.fori_loop(0, n_chunks, k_body, 0)
    own = jnp.sum(q2_ref[0] * knew_ref[0], axis=1, keepdims=True)
    lane = lax.broadcasted_iota(I32, (N_HEADS, PAGE_SIZE), 1)
    p_scr[n_pages] = jnp.where(lane == 0, own, NEG_INF) + bias_ref[n_pages] + mask_ref[0, n_pages:n_pages + 1, :]

    logits = p_scr[...]
    m = jnp.max(jnp.max(logits, axis=0), axis=1, keepdims=True)
    p = jnp.exp(logits - m[None])
    denom = jnp.sum(jnp.sum(p, axis=0), axis=1, keepdims=True)
    p_scr[...] = p

    acc_scr[...] = jnp.zeros(acc_scr.shape, F32)

    def v_body(c, carry):
        s = c % 2
        wait(vcache_ref, vbuf, vsem, c, s)
        for h in range(N_HEADS):
            acc = acc_scr[h]
            for j in range(ppc):
                acc = acc + p_scr[c * ppc + j, h:h + 1, :] * vbuf[s, j, h]
            acc_scr[h] = acc

        @pl.when(c + 2 < n_chunks)
        def _():
            start(vcache_ref, vbuf, vsem, c + 2, s)

        return carry

    lax.fori_loop(0, n_chunks, v_body, 0)
    out = jnp.sum(acc_scr[...], axis=2) + p_scr[n_pages][:, 0:1] * vnew_ref[0]
    o_ref[0] = out / denom


def _sample_attend(layer, page_table, q_s, mask_add, bias_tab, k_new, v_new, cache_k_t, cache_v_t):
    n_b, n_pages = page_table.shape
    n_rows = mask_add.shape[1]
    ppc = _pick_tile(n_pages, (16, 8, 4, 2, 1))
    slab = (N_HEADS, HEAD_DIM, PAGE_SIZE)
    grid_spec = pltpu.PrefetchScalarGridSpec(
        num_scalar_prefetch=1,
        grid=(n_b,),
        in_specs=[
            pl.BlockSpec((1, N_HEADS, HEAD_DIM, 1), lambda b, pt: (b, 0, 0, 0)),
            pl.BlockSpec((1, N_HEADS, HEAD_DIM), lambda b, pt: (b, 0, 0)),
            pl.BlockSpec((1, n_rows, PAGE_SIZE), lambda b, pt: (b, 0, 0)),
            pl.BlockSpec(bias_tab.shape, lambda b, pt: (0, 0, 0)),
            pl.BlockSpec((1, N_HEADS, HEAD_DIM), lambda b, pt: (b, 0, 0)),
            pl.BlockSpec((1, N_HEADS, HEAD_DIM), lambda b, pt: (b, 0, 0)),
            pl.BlockSpec(memory_space=pl.ANY),
            pl.BlockSpec(memory_space=pl.ANY),
        ],
        out_specs=pl.BlockSpec((1, N_HEADS, HEAD_DIM), lambda b, pt: (b, 0, 0)),
        scratch_shapes=[pltpu.VMEM((2, ppc) + slab, F32), pltpu.VMEM((2, ppc) + slab, F32),
                        pltpu.VMEM(slab, F32), pltpu.VMEM((n_pages + 1, N_HEADS, PAGE_SIZE), F32),
                        pltpu.VMEM(slab, F32), pltpu.SemaphoreType.DMA((2,)), pltpu.SemaphoreType.DMA((2,))],
    )
    return pl.pallas_call(
        functools.partial(_sample_attend_kernel, layer, n_pages),
        grid_spec=grid_spec,
        out_shape=jax.ShapeDtypeStruct((n_b, N_HEADS, HEAD_DIM), F32),
        compiler_params=_cparams("arbitrary"),
        name="sample_attend",
    )(page_table, q_s[..., None], q_s, mask_add, bias_tab, k_new, v_new, cache_k_t, cache_v_t)


def _layer_norm(x, g, b):
    mu = jnp.mean(x, axis=-1, keepdims=True)
    xc = x - mu
    var = jnp.mean(xc * xc, axis=-1, keepdims=True)
    return xc * lax.rsqrt(var + LN_EPS) * g + b


def _merge_kernel(alpha, x_ref, ga_ref, gb_ref, gc_ref, ya_ref, yb_ref, yc_ref, pa_ref, pb_ref, pc_ref, wo_ref,
                  g1_ref, b1_ref, wr_ref, br_ref, x1_ref, x1t_ref, te_ref, tg_ref):
    def branch(g_ref, y_ref, p_ref):
        return jax.nn.sigmoid(g_ref[...]) * jnp.dot(y_ref[...], p_ref[...], preferred_element_type=F32)

    merged = branch(ga_ref, ya_ref, pa_ref) + branch(gb_ref, yb_ref, pb_ref) + branch(gc_ref, yc_ref, pc_ref)
    y = alpha * x_ref[...] + jnp.dot(merged.astype(BF16), wo_ref[...], preferred_element_type=F32)
    x1 = _layer_norm(y, g1_ref[...], b1_ref[...])
    x1_ref[...] = x1
    x1t_ref[...] = x1.reshape(x1t_ref.shape)

    dims = (((1,), (1,)), ((), ()))
    logits = lax.dot_general(wr_ref[...], x1.astype(BF16), dims, preferred_element_type=F32) + br_ref[...]
    eidx = lax.broadcasted_iota(I32, logits.shape, 0)
    vals, idxs = [], []
    cur = logits
    for _ in range(TOP_K):
        v = jnp.max(cur, axis=0, keepdims=True)
        i = jnp.min(jnp.where(cur == v, eidx, N_EXPERTS), axis=0, keepdims=True)
        vals.append(v)
        idxs.append(i)
        cur = jnp.where(eidx == i, NEG_INF, cur)
    top_v = jnp.concatenate(vals, axis=0)
    e = jnp.exp(top_v - vals[0])
    tg_ref[...] = e / jnp.sum(e, axis=0, keepdims=True)
    te_ref[...] = jnp.concatenate(idxs, axis=0)


def _merge(alpha, x, z, y_a, y_b, y_c, p_a, p_b, p_c, w_o, ln_g, ln_b, w_r, b_r):
    t, d = x.shape
    cc = y_a.shape[1]
    tm = ROW_TILE

    def const(a):
        return pl.BlockSpec(a.shape, lambda i: (0,) * a.ndim)

    def rows(width, col=0):
        return pl.BlockSpec((tm, width), lambda i, col=col: (i, col))

    return pl.pallas_call(
        functools.partial(_merge_kernel, alpha),
        grid=(t // tm,),
        in_specs=[rows(d), rows(d, 0), rows(d, 1), rows(d, 2), rows(cc), rows(cc), rows(cc),
                  const(p_a), const(p_b), const(p_c), const(w_o), const(ln_g), const(ln_b),
                  const(w_r), const(b_r)],
        out_specs=[rows(d), pl.BlockSpec((tm, d // LANES, LANES), lambda i: (i, 0, 0)),
                   pl.BlockSpec((TOP_K, tm), lambda i: (0, i)), pl.BlockSpec((TOP_K, tm), lambda i: (0, i))],
        out_shape=[jax.ShapeDtypeStruct((t, d), F32), jax.ShapeDtypeStruct((t, d // LANES, LANES), F32),
                   jax.ShapeDtypeStruct((TOP_K, t), I32), jax.ShapeDtypeStruct((TOP_K, t), F32)],
        compiler_params=_cparams("parallel"),
        name="merge_ln_router",
    )(x, z, z, z, y_a, y_b, y_c, p_a, p_b, p_c, w_o, ln_g, ln_b, w_r, b_r)


class _RowGather:
    def __init__(self, idx_ref, src_ref, buf, idx_smem, isem, dsem, rows):
        self.idx_ref, self.src_ref, self.buf = idx_ref, src_ref, buf
        self.idx_smem, self.isem, self.dsem, self.rows = idx_smem, isem, dsem, rows

    def _idx_copy(self, step):
        s = step % 2
        return pltpu.make_async_copy(self.idx_ref.at[pl.ds(step * SUBLANES, SUBLANES), :], self.idx_smem.at[s],
                                     self.isem.at[s])

    def _row_copy(self, token, s, r):
        return pltpu.make_async_copy(self.src_ref.at[token], self.buf.at[s, r], self.dsem.at[s])

    def issue_rows(self, step):
        s = step % 2
        for r in range(self.rows):
            self._row_copy(self.idx_smem[s, r // LANES, r % LANES], s, r).start()
        self._idx_copy(step + 1).start()

    def begin(self, i):
        @pl.when(i == 0)
        def _():
            self._idx_copy(0).start()
            self._idx_copy(0).wait()
            self.issue_rows(0)

        self._idx_copy(i + 1).wait()

    def wait_rows(self, i):
        s = i % 2
        for r in range(self.rows):
            self._row_copy(0, s, r).wait()

    def drain(self, n):
        self.wait_rows(n)
        self._idx_copy(n + 1).wait()


class _RowScatter:
    def __init__(self, idx_ref, dst_ref, buf, idx_smem, isem, dsem, rows, spare_row0):
        self.idx_ref, self.dst_ref, self.buf = idx_ref, dst_ref, buf
        self.idx_smem, self.isem, self.dsem, self.rows = idx_smem, isem, dsem, rows
        self.spare_row0 = spare_row0

    def _idx_copy(self, step):
        s = step % 2
        return pltpu.make_async_copy(self.idx_ref.at[pl.ds(step * SUBLANES, SUBLANES), :], self.idx_smem.at[s],
                                     self.isem.at[s])

    def _row_copy(self, row_id, s, r):
        return pltpu.make_async_copy(self.buf.at[s, r], self.dst_ref.at[row_id], self.dsem.at[s])

    def begin(self, i):
        @pl.when(i == 0)
        def _():
            self.buf[1] = jnp.zeros(self.buf.shape[1:], self.buf.dtype)
            second = self.dst_ref.at[pl.ds(self.spare_row0 + self.rows, self.rows)]
            fill = pltpu.make_async_copy(self.buf.at[1], second, self.dsem.at[1])
            fill.start()
            fill.wait()
            self._idx_copy(0).start()

    def issue(self, i, prefetch=True):
        self._idx_copy(i).wait()
        s = (i + 1) % 2
        for r in range(self.rows):
            self._row_copy(self.idx_smem[i % 2, r // LANES, r % LANES], s, r).start()
        if prefetch:
            self._idx_copy(i + 1).start()

    def wait(self, i):
        s = (i + 1) % 2
        for r in range(self.rows):
            self._row_copy(0, s, r).wait()

    def drain(self, n):
        self.wait(n - 1)
        self.issue(n, prefetch=False)
        self.wait(n)


def _moe_kernel(layer, blk_e_ref, n_used_ref, idx_ref, dst_ref, x_hbm_ref, rg_ref, wg_ref, bg_ref, wu_ref, bu_ref,
                wd_ref, bd_ref, y_hbm_ref, xbuf, ybuf, wbf, idx_smem, dst_smem, isem, dsem, osem, ssem):
    i = pl.program_id(0)
    n_used = n_used_ref[0]
    bm = xbuf.shape[1]
    gather = _RowGather(idx_ref, x_hbm_ref, xbuf, idx_smem, isem, dsem, bm)
    scatter = _RowScatter(dst_ref, y_hbm_ref, ybuf, dst_smem, osem, ssem, bm, y_hbm_ref.shape[0] - 2 * bm)

    @pl.when(i == n_used)
    def _():
        gather.drain(i)
        scatter.drain(i)

    @pl.when(i < n_used)
    def _():
        gather.begin(i)
        scatter.begin(i)
        gather.wait_rows(i)

        @pl.when((i == 0) | (blk_e_ref[i] != blk_e_ref[jnp.maximum(i - 1, 0)]))
        def _():
            wbf[0] = wg_ref[0, 0].astype(BF16)
            wbf[1] = wu_ref[0, 0].astype(BF16)
            wbf[2] = wd_ref[0, 0].astype(BF16)

        gather.issue_rows(i + 1)
        scatter.issue(i)
        xe = xbuf[i % 2].reshape(bm, wbf.shape[1]).astype(BF16)
        g = jnp.minimum(jnp.dot(xe, wbf[0], preferred_element_type=F32) + bg_ref[0, 0], SWIGLU_LIMIT)
        up = jnp.clip(jnp.dot(xe, wbf[1], preferred_element_type=F32) + bu_ref[0, 0], -SWIGLU_LIMIT, SWIGLU_LIMIT)
        h = g * jax.nn.sigmoid(SWIGLU_ALPHA * g) * (up + 1.0)
        y = (jnp.dot(h.astype(BF16), wbf[2], preferred_element_type=F32) + bd_ref[0, 0]) * rg_ref[...]

        @pl.when(i >= 1)
        def _():
            scatter.wait(i - 1)

        ybuf[i % 2] = y.reshape(ybuf.shape[1:])


def _moe_experts(layer, blk_e, n_used, tok_tiles, dst_tiles, n_out_rows, x1t, row_gate, w_gate, b_gate, w_up, b_up,
                 w_down, b_down):
    n_blocks = blk_e.shape[0]
    bm = MOE_BLOCK
    tile = x1t.shape[1:]
    d = tile[0] * tile[1]
    ff = w_gate.shape[3]
    assert d == ff

    def wspec(shape):
        return pl.BlockSpec((1, 1) + shape, lambda i, be, nu: (layer, be[i], 0, 0))

    hbm = pl.BlockSpec(memory_space=pl.ANY)
    grid_spec = pltpu.PrefetchScalarGridSpec(
        num_scalar_prefetch=2,
        grid=(n_blocks,),
        in_specs=[hbm, hbm, hbm,
                  pl.BlockSpec((bm, 1), lambda i, be, nu: (i, 0)),
                  wspec((d, ff)), wspec((1, ff)), wspec((d, ff)), wspec((1, ff)), wspec((ff, d)), wspec((1, d))],
        out_specs=hbm,
        scratch_shapes=[pltpu.VMEM((2, bm) + tile, F32), pltpu.VMEM((2, bm) + tile, F32),
                        pltpu.VMEM((3, d, ff), BF16),
                        pltpu.SMEM((2, SUBLANES, LANES), I32), pltpu.SMEM((2, SUBLANES, LANES), I32),
                        pltpu.SemaphoreType.DMA((2,)), pltpu.SemaphoreType.DMA((2,)),
                        pltpu.SemaphoreType.DMA((2,)), pltpu.SemaphoreType.DMA((2,))],
    )
    return pl.pallas_call(
        functools.partial(_moe_kernel, layer),
        grid_spec=grid_spec,
        out_shape=jax.ShapeDtypeStruct((n_out_rows,) + tile, F32),
        compiler_params=_cparams("arbitrary"),
        name="moe_experts",
    )(blk_e, n_used, tok_tiles, dst_tiles, x1t, row_gate, w_gate, b_gate, w_up, b_up, w_down, b_down)


def _combine_kernel(alpha, x_ref, y_ref, lg_ref, lb_ref, o_ref):
    h = y_ref[:, 0]
    for k in range(1, TOP_K):
        h = h + y_ref[:, k]
    o_ref[...] = _layer_norm(alpha * x_ref[...] + h.reshape(x_ref.shape), lg_ref[...], lb_ref[...])


def _combine(alpha, x1, y_tk, ln_g, ln_b):
    t, d = x1.shape
    tm = ROW_TILE
    return pl.pallas_call(
        functools.partial(_combine_kernel, alpha),
        grid=(t // tm,),
        in_specs=[pl.BlockSpec((tm, d), lambda i: (i, 0)),
                  pl.BlockSpec((tm,) + y_tk.shape[1:], lambda i: (i, 0, 0, 0)),
                  pl.BlockSpec(ln_g.shape, lambda i: (0, 0)),
                  pl.BlockSpec(ln_b.shape, lambda i: (0, 0))],
        out_specs=pl.BlockSpec((tm, d), lambda i: (i, 0)),
        out_shape=jax.ShapeDtypeStruct((t, d), F32),
        compiler_params=_cparams("parallel"),
        name="combine_ln",
    )(x1, y_tk, ln_g, ln_b)


def _moe(alpha, layer, x1, x1t, top_e, top_g, w_gate, b_gate, w_up, b_up, w_down, b_down, ln_g, ln_b):
    t, d = x1.shape
    bm = MOE_BLOCK
    n_assign = t * TOP_K
    assert n_assign % bm == 0
    n_blocks = n_assign // bm + N_EXPERTS
    n_rows = n_blocks * bm
    flat_e = top_e.T.reshape(-1)
    experts = jnp.arange(N_EXPERTS, dtype=I32)
    counts = jnp.sum((flat_e[:, None] == experts[None, :]).astype(I32), axis=0)
    padded = (counts + bm - 1) // bm * bm
    pad_end = jnp.cumsum(padded)
    pad_key = jnp.where(jnp.arange(bm, dtype=I32)[None, :] < (padded - counts)[:, None], experts[:, None], N_EXPERTS)
    keys = jnp.concatenate([flat_e, pad_key.reshape(-1)])
    src = jnp.concatenate([jnp.arange(n_assign, dtype=I32), jnp.full((N_EXPERTS * bm,), n_assign, I32)])
    gates = jnp.concatenate([top_g.T.reshape(-1), jnp.zeros((N_EXPERTS * bm,), F32)])
    _, row_src, row_gate = lax.sort((keys, src, gates), num_keys=1, is_stable=True)
    row_token = jnp.where(row_src < n_assign, row_src // TOP_K, 0)
    row_id = jnp.arange(n_rows, dtype=I32)
    spare = n_assign + ((row_id // bm + 1) % 2) * bm + row_id % bm
    row_dst = jnp.where(row_src < n_assign, row_src, spare)
    spare_tile = n_assign + jnp.arange(bm, dtype=I32)
    n_out_rows = n_assign + 2 * bm
    blk_start = jnp.arange(n_blocks, dtype=I32) * bm
    blk_e = jnp.minimum(jnp.sum((pad_end[None, :] <= blk_start[:, None]).astype(I32), axis=1), N_EXPERTS - 1)
    n_used = (pad_end[-1:] // bm).astype(I32)
    row_gate = row_gate.reshape(n_rows, 1)

    per_tile = SUBLANES * LANES

    def tiles(a):
        return jnp.pad(a, ((0, 2), (0, per_tile - a.shape[1]))).reshape(-1, LANES)

    tok_tiles = tiles(row_token.reshape(n_blocks, bm))
    dst_tiles = tiles(jnp.concatenate([spare_tile[None, :], row_dst.reshape(n_blocks, bm)], axis=0))
    y_rows = _moe_experts(layer, blk_e, n_used, tok_tiles, dst_tiles, n_out_rows, x1t, row_gate,
                          w_gate, b_gate, w_up, b_up, w_down, b_down)
    y_tk = y_rows.reshape((n_out_rows // TOP_K, TOP_K) + y_rows.shape[1:])
    return _combine(alpha, x1, y_tk, ln_g, ln_b)


def _t5_bucket(dist):
    exact = NUM_BUCKETS // 2
    d = jnp.maximum(dist, 1).astype(F32)
    large = exact + (jnp.log(d / exact) / math.log(MAX_DISTANCE / exact) * (NUM_BUCKETS - exact)).astype(I32)
    return jnp.where(dist < exact, dist, jnp.minimum(large, NUM_BUCKETS - 1))


def _toeplitz(v, n):
    length = 2 * n - 1
    hankel = jnp.tile(v, (n + 1, 1))[: n * (length + 1)].reshape(n, length + 1, v.shape[1])[:, :n, :]
    return hankel[::-1].transpose(2, 0, 1)


def _prompt_bias_tables(rel_bias, tile):
    far = rel_bias[NUM_BUCKETS - 1]
    by_dist = (rel_bias[_t5_bucket(jnp.arange(2 * tile, dtype=I32))] - far[None, :]) * LOG2_E
    v0 = jnp.concatenate([jnp.full((tile - 1, by_dist.shape[1]), NEG_INF, F32), by_dist[:tile]], axis=0)
    return jnp.stack([_toeplitz(v0, tile), _toeplitz(by_dist[1:], tile)]).astype(F32)


def kernel(x_prompt, x_sample, cache_k, cache_v, cache_kidx, state_conv, state_pool, page_table, rel_bias,
           w_in, b_in, conv_w, pool_w, pool_scale, p_a, p_b, p_c, w_o, ln1_g, ln1_b,
           w_router, b_router, w_gate, b_gate, w_up, b_up, w_down, b_down, ln2_g, ln2_b):
    n_b, seq, d = x_prompt.shape
    n_s, dec_seq, _ = x_sample.shape
    assert dec_seq == 1
    depth = w_in.shape[0]
    n_pages = page_table.shape[1]
    past = n_pages * PAGE_SIZE
    cc = conv_w.shape[2]
    pw = pool_scale.shape[1]
    aw = N_HEADS * HEAD_DIM
    iw = N_IDX_HEADS * IDX_DIM
    assert cc == pw == aw == iw and seq % ATT_TILE == 0 and d % cc == 0
    alpha = float((2 * depth) ** 0.25)
    n_sel_p = min(TOPK_MAX, seq // 4)
    n_sel_s = min(TOPK_MAX, (past + dec_seq) // 4)

    t_p = n_b * seq
    t_all = _round_up(t_p + n_s, ROW_TILE)
    x = jnp.concatenate([x_prompt.reshape(t_p, d), x_sample.reshape(n_s, d),
                         jnp.zeros((t_all - t_p - n_s, d), F32)], axis=0)

    sizes = (cc, cc, cc, pw, aw, aw, aw, iw, IDX_DIM, N_IDX_HEADS, 3 * d)
    n_in = sum(sizes)
    off_g = n_in - 3 * d
    nz = _round_up(n_in, 1536)
    c_ab = 3 * d
    c_q = c_ab + 4 * cc
    c_ki = c_q + 4 * cc
    bias_tab = _prompt_bias_tables(rel_bias, ATT_TILE)
    pt = page_table.astype(I32)
    key_pos = jnp.arange((n_pages + 1) * PAGE_SIZE, dtype=I32)
    bias_tab_s = rel_bias[_t5_bucket(jnp.clip(past - key_pos, 0, MAX_DISTANCE))]
    bias_tab_s = bias_tab_s.reshape(n_pages + 1, PAGE_SIZE, N_HEADS).transpose(0, 2, 1)
    cache_kidx_t = cache_kidx.transpose(0, 1, 3, 2)
    cache_k_t = cache_k.transpose(0, 1, 3, 4, 2)
    cache_v_t = cache_v.transpose(0, 1, 3, 4, 2)

    outs = {name: [] for name in ("kp", "vp", "kip", "cp", "pp", "ks", "vs", "kis", "cs", "ps")}
    for l in range(depth):
        w_l = jnp.concatenate([w_in[l][:, off_g:], w_in[l][:, :off_g], jnp.zeros((d, nz - n_in), F32)], axis=1)
        b_l = jnp.concatenate([b_in[l][off_g:], b_in[l][:off_g], jnp.zeros((nz - n_in,), F32)])[None, :]
        z = _inproj(x, w_l.astype(BF16), b_l)

        def col(c0, width, rows=slice(None)):
            return z[rows, c0:c0 + width]

        pr = slice(0, t_p)
        sr = slice(t_p, t_p + n_s)
        conv_w_l = conv_w[l]
        pool_w_l = pool_w[l].astype(BF16)
        pool_s_l = pool_scale[l][None, :]

        ya_p, yb_p = _mixer_prompt(z, c_ab, n_b, seq, cc, conv_w_l, pool_w_l, pool_s_l)

        a0 = sum(sizes[:4])
        a1 = a0 + 4 * aw + IDX_DIM + N_IDX_HEADS
        n_att = _round_up(a1 - a0, LANES)
        w_att = jnp.pad(w_in[l][:, a0:a1].T, ((0, n_att - (a1 - a0)), (0, 0))).astype(BF16)
        b_att = jnp.pad(b_in[l][a0:a1], (0, n_att - (a1 - a0)))[:, None]
        q_t, k_t, v_t, v_aug, qi_t, ki_t, w_t = _inproj_t(
            x, w_att, b_att, t_p, HEAD_DIM ** -0.5 * LOG2_E, IDX_DIM ** -0.5, N_IDX_HEADS ** -0.5)
        k_bf = col(c_q + aw, aw, pr).astype(BF16).reshape(n_b, seq, aw)
        ki_bf = col(c_ki, IDX_DIM, pr).astype(BF16).reshape(n_b, seq, IDX_DIM)
        yc_p = _dsa_prompt(qi_t, w_t, ki_bf, q_t, k_bf, v_aug, bias_tab, n_sel_p)

        ya_s, yb_s, u_s = _mixer_sample(
            col(c_ab, cc, sr), col(c_ab + cc, cc, sr), col(c_ab + 2 * cc, cc, sr), col(c_ab + 3 * cc, pw, sr),
            state_conv[l].transpose(1, 0, 2), state_pool[l].transpose(1, 0, 2), past, conv_w_l, pool_w_l, pool_s_l)

        q_s = col(c_q, aw, sr).reshape(n_s, N_HEADS, HEAD_DIM)
        k_s = col(c_q + aw, aw, sr).reshape(n_s, N_HEADS, HEAD_DIM)
        v_s = col(c_q + 2 * aw, aw, sr).reshape(n_s, N_HEADS, HEAD_DIM)
        qi_s = (col(c_q + 3 * aw, iw, sr) * (IDX_DIM ** -0.5)).astype(BF16).reshape(n_s, N_IDX_HEADS, IDX_DIM)
        ki_s = col(c_ki, IDX_DIM, sr)
        wi_s = (col(c_ki + IDX_DIM, N_IDX_HEADS, sr) * (N_IDX_HEADS ** -0.5))[:, :, None]
        ki_new_pad = jnp.pad(ki_s[:, :, None], ((0, 0), (0, 0), (0, PAGE_SIZE - 1)))
        scores = _sample_scores(l, pt, qi_s, wi_s, ki_new_pad, cache_kidx_t)
        mask_add = _sample_select(scores, n_sel_s)
        yc_s = _sample_attend(l, pt, q_s * (HEAD_DIM ** -0.5), mask_add, bias_tab_s, k_s, v_s,
                              cache_k_t, cache_v_t).reshape(n_s, aw)

        pad_rows = t_all - t_p - n_s

        def all_rows(yp, ys_):
            return jnp.concatenate([yp, ys_, jnp.zeros((pad_rows, yp.shape[1]), BF16)], axis=0)

        x1, x1t, top_e, top_g = _merge(
            alpha, x, z, all_rows(ya_p, ya_s), all_rows(yb_p, yb_s),
            all_rows(yc_p.reshape(t_p, aw), yc_s.astype(BF16)),
            p_a[l].astype(BF16), p_b[l].astype(BF16), p_c[l].astype(BF16), w_o[l].astype(BF16),
            ln1_g[l][None, :], ln1_b[l][None, :], w_router[l].T.astype(BF16), b_router[l][:, None])

        x = _moe(alpha, l, x1, x1t, top_e, top_g, w_gate, b_gate[:, :, None, :], w_up, b_up[:, :, None, :],
                 w_down, b_down[:, :, None, :], ln2_g[l][None, :], ln2_b[l][None, :])

        u_p = (col(c_ab + cc, cc, pr) * col(c_ab + 2 * cc, cc, pr)).reshape(n_b, seq, cc)
        up_p = col(c_ab + 3 * cc, pw, pr).reshape(n_b, seq, pw)
        outs["kp"].append(k_t.reshape(N_HEADS, HEAD_DIM, n_b, seq).transpose(2, 3, 0, 1))
        outs["vp"].append(v_t.reshape(N_HEADS, HEAD_DIM, n_b, seq).transpose(2, 3, 0, 1))
        outs["kip"].append(ki_t.reshape(IDX_DIM, n_b, seq).transpose(1, 2, 0))
        outs["cp"].append(u_p[:, seq - (CONV_WIDTH - 1):])
        outs["pp"].append(up_p[:, seq - POOL_STATE:])
        outs["ks"].append(k_s[:, None])
        outs["vs"].append(v_s[:, None])
        outs["kis"].append(ki_s[:, None])
        outs["cs"].append(jnp.concatenate([state_conv[l], u_s[:, None]], axis=1)[:, -(CONV_WIDTH - 1):])
        outs["ps"].append(jnp.concatenate([state_pool[l], col(c_ab + 3 * cc, pw, sr)[:, None]], axis=1)[:, -POOL_STATE:])

    y_prompt = x[:t_p].reshape(n_b, seq, d)
    y_sample = x[t_p:t_p + n_s].reshape(n_s, dec_seq, d)
    stk = {k: jnp.stack(v) for k, v in outs.items()}
    return (y_prompt, y_sample, stk["kp"], stk["vp"], stk["kip"], stk["cp"], stk["pp"],
            stk["ks"], stk["vs"], stk["kis"], stk["cs"], stk["ps"])
```

```python
import functools
import math

import numpy as np
import jax
import jax.numpy as jnp
from jax import lax
from jax.experimental import pallas as pl
from jax.experimental.pallas import tpu as pltpu

F32 = jnp.float32
BF16 = jnp.bfloat16
I32 = jnp.int32

N_HEADS = 8
HEAD_DIM = 64
N_IDX_HEADS = 8
IDX_DIM = 64
TOPK_MAX = 256
POOL_WINDOWS = (2, 4, 8, 16)
POOL_STATE = max(POOL_WINDOWS) - 1
CONV_WIDTH = 3
N_EXPERTS = 32
TOP_K = 4
SWIGLU_LIMIT = 7.0
SWIGLU_ALPHA = 1.702
LN_EPS = 1e-5
NUM_BUCKETS = 32
MAX_DISTANCE = 128
PAGE_SIZE = 128

LANES = 128
SUBLANES = 8
VMEM_LIMIT = 56 * 1024 * 1024

ROW_TILE = 256
ATT_TILE = 256
ATT_PIECES = 2
COUNT_UNROLL = 4
VA_ONES = 16
MOE_BLOCK = 256
HALO = 16
INT_MIN = np.int32(-2 ** 31)
LOG2_E = math.log2(math.e)
NEG_INF = float("-inf")


def _cparams(*sem):
    return pltpu.CompilerParams(dimension_semantics=sem, vmem_limit_bytes=VMEM_LIMIT)


def _round_up(n, m):
    return (n + m - 1) // m * m


def _pick_tile(n, candidates):
    for c in candidates:
        if n % c == 0:
            return c
    raise ValueError(f"no tile for {n} in {candidates}")


def _single(block_shape, index_map):
    return pl.BlockSpec(block_shape, index_map, pipeline_mode=pl.Buffered(1))


def _inproj_kernel(x_ref, w_ref, b_ref, o_ref):
    x = x_ref[...].astype(BF16)
    o_ref[...] = jnp.dot(x, w_ref[...], preferred_element_type=F32) + b_ref[...]


def _inproj(x, w, b):
    m, k = x.shape
    n = w.shape[1]
    tm = _pick_tile(m, (1280, 1024, 768, 512, 256))
    tn = _pick_tile(n, (1536, 1280, 1024, 768, 512, 256, 128))
    return pl.pallas_call(
        _inproj_kernel,
        grid=(n // tn, m // tm),
        in_specs=[
            pl.BlockSpec((tm, k), lambda j, i: (i, 0)),
            pl.BlockSpec((k, tn), lambda j, i: (0, j)),
            pl.BlockSpec((1, tn), lambda j, i: (0, j)),
        ],
        out_specs=pl.BlockSpec((tm, tn), lambda j, i: (i, j)),
        out_shape=jax.ShapeDtypeStruct((m, n), F32),
        compiler_params=_cparams("parallel", "parallel"),
        name="inproj",
    )(x, w, b)


def _inproj_t_kernel(scale_q, scale_qi, scale_w, x_ref, w_ref, b_ref, q_ref, k_ref, v_ref, va_ref, qi_ref, ki_ref,
                     wi_ref, kb_ref, kib_ref):
    x = x_ref[...].astype(BF16)
    aw = q_ref.shape[0]
    dims = (((1,), (1,)), ((), ()))

    def proj(r0, n):
        return lax.dot_general(w_ref[r0:r0 + n, :], x, dims, preferred_element_type=F32) + b_ref[r0:r0 + n, :]

    q_ref[...] = (proj(0, aw) * scale_q).astype(BF16)
    k = proj(aw, aw)
    k_ref[...] = k
    kb_ref[...] = k.T.astype(BF16)
    v = proj(2 * aw, aw)
    v_ref[...] = v
    ones = jnp.ones((va_ref.shape[1] - HEAD_DIM, va_ref.shape[2]), BF16)
    for h in range(N_HEADS):
        va_ref[h, 0:HEAD_DIM, :] = v[h * HEAD_DIM:(h + 1) * HEAD_DIM, :].astype(BF16)
        va_ref[h, HEAD_DIM:, :] = ones
    qi_ref[...] = (proj(3 * aw, aw) * scale_qi).astype(BF16)
    tail = proj(4 * aw, w_ref.shape[0] - 4 * aw)
    ki = tail[0:IDX_DIM, :]
    ki_ref[...] = ki
    kib_ref[...] = ki.T.astype(BF16)
    wi_ref[...] = tail[IDX_DIM:IDX_DIM + N_IDX_HEADS, :] * scale_w


def _inproj_t(x, w_t, b_t, n_b, seq, scale_q, scale_qi, scale_w):
    k = x.shape[1]
    aw = N_HEADS * HEAD_DIM
    tm = _pick_tile(seq, (1024, 512, 256))
    spb = seq // tm
    n_tok = n_b * seq
    va_rows = HEAD_DIM + VA_ONES

    def cols(rows):
        return pl.BlockSpec((rows, tm), lambda i: (0, i))

    def seq_cols(rows):
        return pl.BlockSpec((None, rows, tm), lambda i: (i // spb, 0, i % spb))

    def tok_rows(width):
        return pl.BlockSpec((tm, width), lambda i: (i, 0))

    return pl.pallas_call(
        functools.partial(_inproj_t_kernel, scale_q, scale_qi, scale_w),
        grid=(n_tok // tm,),
        in_specs=[pl.BlockSpec((tm, k), lambda i: (i, 0)),
                  pl.BlockSpec(w_t.shape, lambda i: (0, 0)),
                  pl.BlockSpec(b_t.shape, lambda i: (0, 0))],
        out_specs=[cols(aw), seq_cols(aw), seq_cols(aw), pl.BlockSpec((N_HEADS, va_rows, tm), lambda i: (0, 0, i)),
                   cols(aw), seq_cols(IDX_DIM), cols(N_IDX_HEADS), tok_rows(aw), tok_rows(IDX_DIM)],
        out_shape=[jax.ShapeDtypeStruct((aw, n_tok), BF16), jax.ShapeDtypeStruct((n_b, aw, seq), F32),
                   jax.ShapeDtypeStruct((n_b, aw, seq), F32), jax.ShapeDtypeStruct((N_HEADS, va_rows, n_tok), BF16),
                   jax.ShapeDtypeStruct((aw, n_tok), BF16), jax.ShapeDtypeStruct((n_b, IDX_DIM, seq), F32),
                   jax.ShapeDtypeStruct((N_IDX_HEADS, n_tok), F32), jax.ShapeDtypeStruct((n_tok, aw), BF16),
                   jax.ShapeDtypeStruct((n_tok, IDX_DIM), BF16)],
        compiler_params=_cparams("parallel"),
        name="inproj_t",
    )(x, w_t, b_t)


def _pool_out(diffs, pw_ref, ps_ref):
    outs = [jnp.dot(d.astype(BF16), pw_ref[g], preferred_element_type=F32) for g, d in enumerate(diffs)]
    return jnp.concatenate(outs, axis=-1) * ps_ref[...]


def _mixer_prompt_kernel(ab_ref, ac_ref, ah_ref, up_ref, cw_ref, pw_ref, ps_ref, ya_ref, yb_ref, eu_ref, ep_ref):
    si = pl.program_id(1)
    ts = ab_ref.shape[0]
    pg = up_ref.shape[1] // len(POOL_WINDOWS)

    @pl.when(si == 0)
    def _():
        eu_ref[0:HALO, :] = jnp.zeros((HALO, eu_ref.shape[1]), F32)
        ep_ref[0:HALO, :] = jnp.zeros((HALO, ep_ref.shape[1]), F32)

    @pl.when(si > 0)
    def _():
        eu_ref[0:HALO, :] = eu_ref[ts:ts + HALO, :]
        ep_ref[0:HALO, :] = ep_ref[ts:ts + HALO, :]

    u = ac_ref[...] * ah_ref[...]
    eu_ref[HALO:HALO + ts, :] = u
    ep_ref[HALO:HALO + ts, :] = up_ref[...]
    conv = cw_ref[0:1, :] * eu_ref[HALO - 2:HALO - 2 + ts, :]
    conv = conv + cw_ref[1:2, :] * eu_ref[HALO - 1:HALO - 1 + ts, :]
    conv = conv + cw_ref[2:3, :] * u
    ya_ref[...] = (ab_ref[...] * conv).astype(ya_ref.dtype)

    pos = si * ts + lax.broadcasted_iota(I32, (ts, 1), 0)
    diffs = []
    for g, w in enumerate(POOL_WINDOWS):
        sl = slice(g * pg, (g + 1) * pg)
        win = ep_ref[HALO:HALO + ts, sl]
        for j in range(1, w):
            win = win + ep_ref[HALO - j:HALO - j + ts, sl]
        cnt = jnp.minimum(w, pos + 1).astype(F32)
        diffs.append(win / cnt - up_ref[:, sl])
    yb_ref[...] = _pool_out(diffs, pw_ref, ps_ref).astype(yb_ref.dtype)


def _mixer_prompt(z, col0, n_b, seq, cc, conv_w, pool_w_bf, pool_scale):
    ts = _pick_tile(seq, (512, 256, 128))
    nblk = seq // ts
    cb = col0 // cc

    def zspec(i):
        return pl.BlockSpec((ts, cc), lambda b, s, i=i: (b * nblk + s, cb + i))

    out_spec = pl.BlockSpec((ts, cc), lambda b, s: (b * nblk + s, 0))
    return pl.pallas_call(
        _mixer_prompt_kernel,
        grid=(n_b, nblk),
        in_specs=[zspec(0), zspec(1), zspec(2), zspec(3),
                  pl.BlockSpec(conv_w.shape, lambda b, s: (0, 0)),
                  pl.BlockSpec(pool_w_bf.shape, lambda b, s: (0, 0, 0)),
                  pl.BlockSpec(pool_scale.shape, lambda b, s: (0, 0))],
        out_specs=[out_spec, out_spec],
        out_shape=[jax.ShapeDtypeStruct((n_b * seq, cc), BF16)] * 2,
        scratch_shapes=[pltpu.VMEM((HALO + ts, cc), F32), pltpu.VMEM((HALO + ts, cc), F32)],
        compiler_params=_cparams("arbitrary", "arbitrary"),
        name="mixer_prompt",
    )(z, z, z, z, conv_w, pool_w_bf, pool_scale)


def _mixer_sample_kernel(pos0, ab_ref, ac_ref, ah_ref, up_ref, cprev_ref, pprev_ref, cw_ref, pw_ref, ps_ref,
                         ya_ref, yb_ref, u_ref):
    pg = up_ref.shape[1] // len(POOL_WINDOWS)
    u = ac_ref[...] * ah_ref[...]
    u_ref[...] = u
    conv = cw_ref[0:1, :] * cprev_ref[0] + cw_ref[1:2, :] * cprev_ref[1] + cw_ref[2:3, :] * u
    ya_ref[...] = (ab_ref[...] * conv).astype(ya_ref.dtype)
    diffs = []
    for g, w in enumerate(POOL_WINDOWS):
        sl = slice(g * pg, (g + 1) * pg)
        win = up_ref[:, sl]
        for j in range(1, w):
            win = win + pprev_ref[POOL_STATE - j, :, sl]
        cnt = float(min(w, pos0 + 1))
        diffs.append(win / cnt - up_ref[:, sl])
    yb_ref[...] = _pool_out(diffs, pw_ref, ps_ref).astype(yb_ref.dtype)


def _mixer_sample(a_b, a_c, a_h, u_pool, conv_prev_t, pool_prev_t, pos0, conv_w, pool_w_bf, pool_scale):
    n, cc = a_b.shape
    return pl.pallas_call(
        functools.partial(_mixer_sample_kernel, pos0),
        out_shape=[jax.ShapeDtypeStruct((n, cc), BF16), jax.ShapeDtypeStruct((n, cc), BF16),
                   jax.ShapeDtypeStruct((n, cc), F32)],
        compiler_params=pltpu.CompilerParams(vmem_limit_bytes=VMEM_LIMIT),
        name="mixer_sample",
    )(a_b, a_c, a_h, u_pool, conv_prev_t, pool_prev_t, conv_w, pool_w_bf, pool_scale)


def _sortable_key(score):
    bits = lax.bitcast_convert_type(score, I32)
    return bits ^ ((bits >> 31) & np.int32(0x7FFFFFFF))


def _kth_largest_key(count_ge, k, n_total, shape):
    def cond(st):
        it, _, cnt_cur = st
        return (it < 32) & jnp.any(cnt_cur != k)

    def body(st):
        it, thr_u, cnt_cur = st
        cand_u = thr_u | jnp.left_shift(jnp.int32(1), 31 - it)
        cnt = count_ge(cand_u ^ INT_MIN)
        ok = cnt >= k
        return it + 1, jnp.where(ok, cand_u, thr_u), jnp.where(ok, cnt, cnt_cur)

    init = (jnp.int32(0), jnp.zeros(shape, I32), jnp.zeros(shape, I32) + n_total)
    _, thr_u, cnt_cur = lax.while_loop(cond, body, init)
    return thr_u ^ INT_MIN, cnt_cur


def _dsa_prompt_kernel(n_sel, qi_t_ref, w_t_ref, kib_ref, q_t_ref, k_ref, va_ref, bias_ref, o_ref,
                       key_scr, mask_scr, qz_scr, acc_scr, m_scr, alpha_scr, lg_scr, p_scr):
    qi = pl.program_id(1)
    tq = o_ref.shape[0]
    tk = tq
    n_heads = N_HEADS
    dh = HEAD_DIM

    def score_chunk(c):
        r0 = pl.multiple_of(c * tk, tk)
        kc = kib_ref[pl.ds(r0, tk), :]
        acc = jnp.zeros((tk, tq), F32)
        for h in range(N_IDX_HEADS):
            d = jnp.dot(kc, qi_t_ref[h * IDX_DIM:(h + 1) * IDX_DIM, :], preferred_element_type=F32)
            acc = acc + w_t_ref[h:h + 1, :] * jnp.maximum(d, 0.0)
        return acc

    def score_body(c, carry):
        key_scr[c] = _sortable_key(score_chunk(c))
        return carry

    lax.fori_loop(0, qi, score_body, 0)
    key_pos = lax.broadcasted_iota(I32, (tk, tq), 0)
    qry_pos = lax.broadcasted_iota(I32, (tk, tq), 1)
    diag = jnp.where(key_pos <= qry_pos, score_chunk(qi), NEG_INF)
    key_scr[qi] = _sortable_key(diag)

    def count_where(pred):
        def chunk_count(c):
            ind = jnp.where(pred(key_scr[c], c), 1, 0).astype(I32)
            return ind.reshape(tk // SUBLANES, SUBLANES, tq).sum(axis=0)

        def group_body(g, accs):
            return tuple(acc + chunk_count(COUNT_UNROLL * g + u) for u, acc in enumerate(accs))

        n_groups = (qi + 1) // COUNT_UNROLL
        zero = jnp.zeros((SUBLANES, tq), I32)
        accs = lax.fori_loop(0, n_groups, group_body, (zero,) * COUNT_UNROLL)
        acc = functools.reduce(lambda a, b: a + b, accs)
        acc = lax.fori_loop(n_groups * COUNT_UNROLL, qi + 1, lambda c, a: a + chunk_count(c), acc)
        return acc.sum(axis=0, keepdims=True)

    thr, cnt_ge = _kth_largest_key(lambda cand: count_where(lambda blk, c: blk >= cand), n_sel,
                                   (qi + 1) * tk, (1, tq))

    neg_inf_key = _sortable_key(jnp.full((1, tq), NEG_INF, F32))
    overflow = (cnt_ge > n_sel) & (thr > neg_inf_key)

    @pl.when(jnp.any(overflow))
    def _():
        need = n_sel - count_where(lambda blk, c: blk > thr)

        def count_tied_below(bound):
            return count_where(lambda blk, c: (blk == thr) & (c * tk + key_pos < bound))

        def bis_body(it, lohi):
            lo, hi = lohi
            mid = (lo + hi) // 2
            ok = count_tied_below(mid) >= need
            return jnp.where(ok, lo, mid), jnp.where(ok, mid, hi)

        n_keys = (qi + 1) * tk
        lo0 = jnp.zeros((1, tq), I32)
        hi0 = jnp.zeros((1, tq), I32) + n_keys
        n_it = int(math.ceil(math.log2(key_scr.shape[0] * tk))) + 1
        _, bound = lax.fori_loop(0, n_it, bis_body, (lo0, hi0))

        def demote(c, carry):
            blk = key_scr[c]
            drop = overflow & (blk == thr) & (c * tk + key_pos >= bound)
            key_scr[c] = jnp.where(drop, thr - 1, blk)
            return carry

        lax.fori_loop(0, qi + 1, demote, 0)

    zero_half = jnp.zeros((dh, tq), BF16)
    for h in range(n_heads):
        qh = q_t_ref[h * dh:(h + 1) * dh, :]
        pair = (qh, zero_half) if h % 2 == 0 else (zero_half, qh)
        qz_scr[h] = jnp.concatenate(pair, axis=0)
    pk = tk // ATT_PIECES
    m_scr[...] = jnp.full(m_scr.shape, -1e30, F32)
    acc_scr[...] = jnp.zeros(acc_scr.shape, F32)

    def attend(c, bias_idx):
        c0 = pl.multiple_of(c * tk, tk)
        mask_scr[...] = jnp.where(key_scr[c] >= thr, 0.0, NEG_INF).astype(F32)

        def logits_piece(h, j):
            p2 = h // 2
            rows = slice(j * pk, (j + 1) * pk)
            kc = k_ref[pl.ds(pl.multiple_of(c0 + j * pk, pk), pk), p2 * 2 * dh:(p2 + 1) * 2 * dh]
            logit = jnp.dot(kc, qz_scr[h], preferred_element_type=F32) + mask_scr[rows, :]
            if bias_idx is not None:
                logit = logit + bias_ref[bias_idx, h, rows, :]
            lg_scr[h, rows, :] = logit
            return logit.reshape(pk // SUBLANES, SUBLANES, tq).max(axis=0)

        def finish_max(h, cm):
            m_old = m_scr[h]
            m_new = jnp.maximum(m_old, jnp.max(cm, axis=0, keepdims=True))
            m_scr[h] = m_new
            alpha_scr[h] = jnp.exp2(m_old - m_new)

        def exp_piece(h, j):
            rows = slice(j * pk, (j + 1) * pk)
            p_scr[h, rows, :] = jnp.exp2(lg_scr[h, rows, :].reshape(pk // SUBLANES, SUBLANES, tq) - m_scr[h][None]
                                         ).reshape(pk, tq).astype(BF16)

        def value_step(h):
            pv = jnp.dot(va_ref[h, :, pl.ds(c0, tk)], p_scr[h], preferred_element_type=F32)
            acc_scr[h] = alpha_scr[h, 0:1, :] * acc_scr[h] + pv

        for h in range(n_heads):
            cm = None
            for j in range(ATT_PIECES):
                pm = logits_piece(h, j)
                cm = pm if cm is None else jnp.maximum(cm, pm)
            finish_max(h, cm)
        for h in range(n_heads):
            for j in range(ATT_PIECES):
                exp_piece(h, j)
        for h in range(n_heads):
            value_step(h)

    def far_body(c, carry):
        attend(c, None)
        return carry

    lax.fori_loop(0, jnp.maximum(qi - 1, 0), far_body, 0)

    @pl.when(qi >= 1)
    def _():
        attend(qi - 1, 1)

    attend(qi, 0)
    rows = [acc_scr[h, 0:dh, :] / acc_scr[h, dh:dh + 1, :] for h in range(n_heads)]
    o_ref[...] = jnp.concatenate(rows, axis=0).T.astype(o_ref.dtype)


def _dsa_prompt(qi_t, w_t, kib, q_t, k_bf, v_aug, bias_tab, n_sel):
    n_b, seq, width = k_bf.shape
    tq = ATT_TILE
    nq = seq // tq
    va_rows = v_aug.shape[1]
    kernel = functools.partial(_dsa_prompt_kernel, n_sel)
    return pl.pallas_call(
        kernel,
        grid=(n_b, nq),
        in_specs=[
            pl.BlockSpec((qi_t.shape[0], tq), lambda b, q: (0, b * nq + q)),
            pl.BlockSpec((w_t.shape[0], tq), lambda b, q: (0, b * nq + q)),
            _single((None, seq, kib.shape[2]), lambda b, q: (b, 0, 0)),
            pl.BlockSpec((width, tq), lambda b, q: (0, b * nq + q)),
            _single((None, seq, width), lambda b, q: (b, 0, 0)),
            _single((N_HEADS, va_rows, seq), lambda b, q: (0, 0, b)),
            _single(bias_tab.shape, lambda b, q: (0, 0, 0, 0)),
        ],
        out_specs=pl.BlockSpec((None, tq, width), lambda b, q: (b, q, 0)),
        out_shape=jax.ShapeDtypeStruct((n_b, seq, width), BF16),
        scratch_shapes=[
            pltpu.VMEM((nq, tq, tq), I32),
            pltpu.VMEM((tq, tq), F32),
            pltpu.VMEM((N_HEADS, 2 * HEAD_DIM, tq), BF16),
            pltpu.VMEM((N_HEADS, va_rows, tq), F32),
            pltpu.VMEM((N_HEADS, SUBLANES, tq), F32),
            pltpu.VMEM((N_HEADS, SUBLANES, tq), F32),
            pltpu.VMEM((N_HEADS, tq, tq), F32),
            pltpu.VMEM((N_HEADS, tq, tq), BF16),
        ],
        compiler_params=_cparams("arbitrary", "arbitrary"),
        name="dsa_prompt",
    )(qi_t, w_t, kib, q_t, k_bf, v_aug, bias_tab)


def _sample_score_kernel(layer, n_pages, pt_ref, qi_ref, w_ref, knew_ref, kcache_ref, o_ref, kbuf, sem):
    b = pl.program_id(0)
    nb = pl.num_programs(0)
    slot = b % 2

    def page_copy(seq_id, p, s):
        return pltpu.make_async_copy(kcache_ref.at[layer, pt_ref[seq_id, p]], kbuf.at[s, p], sem.at[s])

    def start_all(seq_id, s):
        def body(p, c):
            page_copy(seq_id, p, s).start()
            return c

        lax.fori_loop(0, n_pages, body, 0)

    @pl.when(b == 0)
    def _():
        start_all(0, 0)

    @pl.when(b + 1 < nb)
    def _():
        start_all(b + 1, 1 - slot)

    def wait(p, c):
        page_copy(b, p, slot).wait()
        return c

    lax.fori_loop(0, n_pages, wait, 0)

    qi = qi_ref[0]
    w = w_ref[0]

    def score_rows(keys_t):
        d = jnp.dot(qi, keys_t.astype(BF16), preferred_element_type=F32)
        return jnp.sum(w * jnp.maximum(d, 0.0), axis=0, keepdims=True)

    group = _pick_tile(n_pages, (8, 4, 2, 1))

    def page_body(g, c):
        p0 = pl.multiple_of(g * group, group)
        rows = [score_rows(kbuf[slot, p0 + j]) for j in range(group)]
        o_ref[0, pl.ds(p0, group), :] = jnp.concatenate(rows, axis=0)
        return c

    lax.fori_loop(0, n_pages // group, page_body, 0)
    own = score_rows(knew_ref[0])
    lane = lax.broadcasted_iota(I32, own.shape, 1)
    o_ref[0, n_pages:n_pages + 1, :] = jnp.where(lane == 0, own, NEG_INF)
    n_rows = o_ref.shape[1]
    if n_rows > n_pages + 1:
        o_ref[0, n_pages + 1:, :] = jnp.full((n_rows - n_pages - 1, PAGE_SIZE), NEG_INF, F32)


def _sample_scores(layer, page_table, qi_s, w_s, kidx_new_pad, cache_kidx_t):
    n_b, n_pages = page_table.shape
    n_rows = _round_up(n_pages + 1, SUBLANES)
    grid_spec = pltpu.PrefetchScalarGridSpec(
        num_scalar_prefetch=1,
        grid=(n_b,),
        in_specs=[
            pl.BlockSpec((1, N_IDX_HEADS, IDX_DIM), lambda b, pt: (b, 0, 0)),
            pl.BlockSpec((1, N_IDX_HEADS, 1), lambda b, pt: (b, 0, 0)),
            pl.BlockSpec((1, IDX_DIM, PAGE_SIZE), lambda b, pt: (b, 0, 0)),
            pl.BlockSpec(memory_space=pl.ANY),
        ],
        out_specs=pl.BlockSpec((1, n_rows, PAGE_SIZE), lambda b, pt: (b, 0, 0)),
        scratch_shapes=[pltpu.VMEM((2, n_pages, IDX_DIM, PAGE_SIZE), F32), pltpu.SemaphoreType.DMA((2,))],
    )
    return pl.pallas_call(
        functools.partial(_sample_score_kernel, layer, n_pages),
        grid_spec=grid_spec,
        out_shape=jax.ShapeDtypeStruct((n_b, n_rows, PAGE_SIZE), F32),
        compiler_params=_cparams("arbitrary"),
        name="sample_scores",
    )(page_table, qi_s, w_s, kidx_new_pad, cache_kidx_t)


def _sample_select_kernel(n_sel, s_ref, o_ref):
    score = s_ref[...]
    n_b, n_rows, _ = score.shape
    key = _sortable_key(score)
    pos = lax.broadcasted_iota(I32, key.shape, 1) * PAGE_SIZE + lax.broadcasted_iota(I32, key.shape, 2)

    def count(pred):
        ones = jnp.where(pred, 1, 0).astype(I32)
        return jnp.sum(jnp.sum(ones, axis=1, keepdims=True), axis=2, keepdims=True)

    thr, _ = _kth_largest_key(lambda cand: count(key >= cand), n_sel, n_rows * PAGE_SIZE, (n_b, 1, 1))
    need = n_sel - count(key > thr)

    def bis_body(it, lohi):
        lo, hi = lohi
        mid = (lo + hi) // 2
        ok = count((key == thr) & (pos < mid)) >= need
        return jnp.where(ok, lo, mid), jnp.where(ok, mid, hi)

    n_it = int(math.ceil(math.log2(n_rows * PAGE_SIZE))) + 1
    _, bound = lax.fori_loop(0, n_it, bis_body,
                             (jnp.zeros((n_b, 1, 1), I32), jnp.full((n_b, 1, 1), n_rows * PAGE_SIZE, I32)))
    sel = (key > thr) | ((key == thr) & (pos < bound))
    o_ref[...] = jnp.where(sel, 0.0, NEG_INF).astype(F32)


def _sample_select(scores, n_sel):
    return pl.pallas_call(
        functools.partial(_sample_select_kernel, n_sel),
        out_shape=jax.ShapeDtypeStruct(scores.shape, F32),
        compiler_params=pltpu.CompilerParams(vmem_limit_bytes=VMEM_LIMIT),
        name="sample_select",
    )(scores)


def _sample_attend_kernel(layer, n_pages, pt_ref, q3_ref, q2_ref, mask_ref, bias_ref, knew_ref, vnew_ref,
                          kcache_ref, vcache_ref, o_ref, kbuf, vbuf, qb_scr, p_scr, acc_scr, ksem, vsem):
    b = pl.program_id(0)
    ppc = kbuf.shape[1]
    n_chunks = n_pages // ppc

    def page_copy(cache_ref, buf, sem, c, j, s):
        return pltpu.make_async_copy(cache_ref.at[layer, pt_ref[b, c * ppc + j]], buf.at[s, j], sem.at[s])

    def start(cache_ref, buf, sem, c, s):
        for j in range(ppc):
            page_copy(cache_ref, buf, sem, c, j, s).start()

    def wait(cache_ref, buf, sem, c, s):
        for j in range(ppc):
            page_copy(cache_ref, buf, sem, c, j, s).wait()

    for c0 in range(min(2, n_chunks)):
        start(kcache_ref, kbuf, ksem, c0, c0)
        start(vcache_ref, vbuf, vsem, c0, c0)
    qb_scr[...] = jnp.broadcast_to(q3_ref[0], qb_scr.shape)

    def k_body(c, carry):
        s = c % 2
        wait(kcache_ref, kbuf, ksem, c, s)
        for j in range(ppc):
            page = c * ppc + j
            logit = jnp.sum(kbuf[s, j] * qb_scr[...], axis=1)
            p_scr[page] = logit + bias_ref[page] + mask_ref[0, pl.ds(page, 1), :]

        @pl.when(c + 2 < n_chunks)
        def _():
            start(kcache_ref, kbuf, ksem, c + 2, s)

        return carry

    lax.fori_loop(0, n_chunks, k_body, 0)
    own = jnp.sum(q2_ref[0] * knew_ref[0], axis=1, keepdims=True)
    lane = lax.broadcasted_iota(I32, (N_HEADS, PAGE_SIZE), 1)
    p_scr[n_pages] = jnp.where(lane == 0, own, NEG_INF) + bias_ref[n_pages] + mask_ref[0, n_pages:n_pages + 1, :]

    logits = p_scr[...]
    m = jnp.max(jnp.max(logits, axis=0), axis=1, keepdims=True)
    p = jnp.exp(logits - m[None])
    denom = jnp.sum(jnp.sum(p, axis=0), axis=1, keepdims=True)
    p_scr[...] = p

    acc_scr[...] = jnp.zeros(acc_scr.shape, F32)

    def v_body(c, carry):
        s = c % 2
        wait(vcache_ref, vbuf, vsem, c, s)
        for h in range(N_HEADS):
            acc = acc_scr[h]
            for j in range(ppc):
                acc = acc + p_scr[c * ppc + j, h:h + 1, :] * vbuf[s, j, h]
            acc_scr[h] = acc

        @pl.when(c + 2 < n_chunks)
        def _():
            start(vcache_ref, vbuf, vsem, c + 2, s)

        return carry

    lax.fori_loop(0, n_chunks, v_body, 0)
    out = jnp.sum(acc_scr[...], axis=2) + p_scr[n_pages][:, 0:1] * vnew_ref[0]
    o_ref[0] = out / denom


def _sample_attend(layer, page_table, q_s, mask_add, bias_tab, k_new, v_new, cache_k_t, cache_v_t):
    n_b, n_pages = page_table.shape
    n_rows = mask_add.shape[1]
    ppc = _pick_tile(n_pages, (16, 8, 4, 2, 1))
    slab = (N_HEADS, HEAD_DIM, PAGE_SIZE)
    grid_spec = pltpu.PrefetchScalarGridSpec(
        num_scalar_prefetch=1,
        grid=(n_b,),
        in_specs=[
            pl.BlockSpec((1, N_HEADS, HEAD_DIM, 1), lambda b, pt: (b, 0, 0, 0)),
            pl.BlockSpec((1, N_HEADS, HEAD_DIM), lambda b, pt: (b, 0, 0)),
            pl.BlockSpec((1, n_rows, PAGE_SIZE), lambda b, pt: (b, 0, 0)),
            pl.BlockSpec(bias_tab.shape, lambda b, pt: (0, 0, 0)),
            pl.BlockSpec((1, N_HEADS, HEAD_DIM), lambda b, pt: (b, 0, 0)),
            pl.BlockSpec((1, N_HEADS, HEAD_DIM), lambda b, pt: (b, 0, 0)),
            pl.BlockSpec(memory_space=pl.ANY),
            pl.BlockSpec(memory_space=pl.ANY),
        ],
        out_specs=pl.BlockSpec((1, N_HEADS, HEAD_DIM), lambda b, pt: (b, 0, 0)),
        scratch_shapes=[pltpu.VMEM((2, ppc) + slab, F32), pltpu.VMEM((2, ppc) + slab, F32),
                        pltpu.VMEM(slab, F32), pltpu.VMEM((n_pages + 1, N_HEADS, PAGE_SIZE), F32),
                        pltpu.VMEM(slab, F32), pltpu.SemaphoreType.DMA((2,)), pltpu.SemaphoreType.DMA((2,))],
    )
    return pl.pallas_call(
        functools.partial(_sample_attend_kernel, layer, n_pages),
        grid_spec=grid_spec,
        out_shape=jax.ShapeDtypeStruct((n_b, N_HEADS, HEAD_DIM), F32),
        compiler_params=_cparams("arbitrary"),
        name="sample_attend",
    )(page_table, q_s[..., None], q_s, mask_add, bias_tab, k_new, v_new, cache_k_t, cache_v_t)


def _layer_norm(x, g, b):
    mu = jnp.mean(x, axis=-1, keepdims=True)
    xc = x - mu
    var = jnp.mean(xc * xc, axis=-1, keepdims=True)
    return xc * lax.rsqrt(var + LN_EPS) * g + b


def _merge_kernel(alpha, x_ref, ga_ref, gb_ref, gc_ref, ya_ref, yb_ref, yc_ref, pa_ref, pb_ref, pc_ref, wo_ref,
                  g1_ref, b1_ref, wr_ref, br_ref, x1_ref, x1t_ref, te_ref, tg_ref):
    def branch(g_ref, y_ref, p_ref):
        return jax.nn.sigmoid(g_ref[...]) * jnp.dot(y_ref[...], p_ref[...], preferred_element_type=F32)

    merged = branch(ga_ref, ya_ref, pa_ref) + branch(gb_ref, yb_ref, pb_ref) + branch(gc_ref, yc_ref, pc_ref)
    y = alpha * x_ref[...] + jnp.dot(merged.astype(BF16), wo_ref[...], preferred_element_type=F32)
    x1 = _layer_norm(y, g1_ref[...], b1_ref[...])
    x1_ref[...] = x1
    x1t_ref[...] = x1.reshape(x1t_ref.shape)

    dims = (((1,), (1,)), ((), ()))
    logits = lax.dot_general(wr_ref[...], x1.astype(BF16), dims, preferred_element_type=F32) + br_ref[...]
    eidx = lax.broadcasted_iota(I32, logits.shape, 0)
    vals, idxs = [], []
    cur = logits
    for _ in range(TOP_K):
        v = jnp.max(cur, axis=0, keepdims=True)
        i = jnp.min(jnp.where(cur == v, eidx, N_EXPERTS), axis=0, keepdims=True)
        vals.append(v)
        idxs.append(i)
        cur = jnp.where(eidx == i, NEG_INF, cur)
    top_v = jnp.concatenate(vals, axis=0)
    e = jnp.exp(top_v - vals[0])
    tg_ref[...] = e / jnp.sum(e, axis=0, keepdims=True)
    te_ref[...] = jnp.concatenate(idxs, axis=0)


def _merge(alpha, x, z, y_a, y_b, y_c, p_a, p_b, p_c, w_o, ln_g, ln_b, w_r, b_r):
    t, d = x.shape
    cc = y_a.shape[1]
    tm = ROW_TILE

    def const(a):
        return pl.BlockSpec(a.shape, lambda i: (0,) * a.ndim)

    def rows(width, col=0):
        return pl.BlockSpec((tm, width), lambda i, col=col: (i, col))

    return pl.pallas_call(
        functools.partial(_merge_kernel, alpha),
        grid=(t // tm,),
        in_specs=[rows(d), rows(d, 0), rows(d, 1), rows(d, 2), rows(cc), rows(cc), rows(cc),
                  const(p_a), const(p_b), const(p_c), const(w_o), const(ln_g), const(ln_b),
                  const(w_r), const(b_r)],
        out_specs=[rows(d), pl.BlockSpec((tm, d // LANES, LANES), lambda i: (i, 0, 0)),
                   pl.BlockSpec((TOP_K, tm), lambda i: (0, i)), pl.BlockSpec((TOP_K, tm), lambda i: (0, i))],
        out_shape=[jax.ShapeDtypeStruct((t, d), F32), jax.ShapeDtypeStruct((t, d // LANES, LANES), F32),
                   jax.ShapeDtypeStruct((TOP_K, t), I32), jax.ShapeDtypeStruct((TOP_K, t), F32)],
        compiler_params=_cparams("parallel"),
        name="merge_ln_router",
    )(x, z, z, z, y_a, y_b, y_c, p_a, p_b, p_c, w_o, ln_g, ln_b, w_r, b_r)


class _RowGather:
    def __init__(self, idx_ref, src_ref, buf, idx_smem, isem, dsem, rows):
        self.idx_ref, self.src_ref, self.buf = idx_ref, src_ref, buf
        self.idx_smem, self.isem, self.dsem, self.rows = idx_smem, isem, dsem, rows

    def _idx_copy(self, step):
        s = step % 2
        return pltpu.make_async_copy(self.idx_ref.at[pl.ds(step * SUBLANES, SUBLANES), :], self.idx_smem.at[s],
                                     self.isem.at[s])

    def _row_copy(self, token, s, r):
        return pltpu.make_async_copy(self.src_ref.at[token], self.buf.at[s, r], self.dsem.at[s])

    def issue_rows(self, step):
        s = step % 2
        for r in range(self.rows):
            self._row_copy(self.idx_smem[s, r // LANES, r % LANES], s, r).start()
        self._idx_copy(step + 1).start()

    def begin(self, i):
        @pl.when(i == 0)
        def _():
            self._idx_copy(0).start()
            self._idx_copy(0).wait()
            self.issue_rows(0)

        self._idx_copy(i + 1).wait()

    def wait_rows(self, i):
        s = i % 2
        for r in range(self.rows):
            self._row_copy(0, s, r).wait()

    def drain(self, n):
        self.wait_rows(n)
        self._idx_copy(n + 1).wait()


class _RowScatter:
    def __init__(self, idx_ref, dst_ref, buf, idx_smem, isem, dsem, rows, spare_row0):
        self.idx_ref, self.dst_ref, self.buf = idx_ref, dst_ref, buf
        self.idx_smem, self.isem, self.dsem, self.rows = idx_smem, isem, dsem, rows
        self.spare_row0 = spare_row0

    def _idx_copy(self, step):
        s = step % 2
        return pltpu.make_async_copy(self.idx_ref.at[pl.ds(step * SUBLANES, SUBLANES), :], self.idx_smem.at[s],
                                     self.isem.at[s])

    def _row_copy(self, row_id, s, r):
        return pltpu.make_async_copy(self.buf.at[s, r], self.dst_ref.at[row_id], self.dsem.at[s])

    def begin(self, i):
        @pl.when(i == 0)
        def _():
            self.buf[1] = jnp.zeros(self.buf.shape[1:], self.buf.dtype)
            second = self.dst_ref.at[pl.ds(self.spare_row0 + self.rows, self.rows)]
            fill = pltpu.make_async_copy(self.buf.at[1], second, self.dsem.at[1])
            fill.start()
            fill.wait()
            self._idx_copy(0).start()

    def issue(self, i, prefetch=True):
        self._idx_copy(i).wait()
        s = (i + 1) % 2
        for r in range(self.rows):
            self._row_copy(self.idx_smem[i % 2, r // LANES, r % LANES], s, r).start()
        if prefetch:
            self._idx_copy(i + 1).start()

    def wait(self, i):
        s = (i + 1) % 2
        for r in range(self.rows):
            self._row_copy(0, s, r).wait()

    def drain(self, n):
        self.wait(n - 1)
        self.issue(n, prefetch=False)
        self.wait(n)


def _moe_kernel(layer, blk_e_ref, n_used_ref, idx_ref, dst_ref, x_hbm_ref, rg_ref, wg_ref, bg_ref, wu_ref, bu_ref,
                wd_ref, bd_ref, y_hbm_ref, xbuf, ybuf, wbf, idx_smem, dst_smem, isem, dsem, osem, ssem):
    i = pl.program_id(0)
    n_used = n_used_ref[0]
    bm = xbuf.shape[1]
    gather = _RowGather(idx_ref, x_hbm_ref, xbuf, idx_smem, isem, dsem, bm)
    scatter = _RowScatter(dst_ref, y_hbm_ref, ybuf, dst_smem, osem, ssem, bm, y_hbm_ref.shape[0] - 2 * bm)

    @pl.when(i == n_used)
    def _():
        gather.drain(i)
        scatter.drain(i)

    @pl.when(i < n_used)
    def _():
        gather.begin(i)
        scatter.begin(i)
        gather.wait_rows(i)

        @pl.when((i == 0) | (blk_e_ref[i] != blk_e_ref[jnp.maximum(i - 1, 0)]))
        def _():
            wbf[0] = wg_ref[0, 0].astype(BF16)
            wbf[1] = wu_ref[0, 0].astype(BF16)
            wbf[2] = wd_ref[0, 0].astype(BF16)

        gather.issue_rows(i + 1)
        scatter.issue(i)
        xe = xbuf[i % 2].reshape(bm, wbf.shape[1]).astype(BF16)
        g = jnp.minimum(jnp.dot(xe, wbf[0], preferred_element_type=F32) + bg_ref[0, 0], SWIGLU_LIMIT)
        up = jnp.clip(jnp.dot(xe, wbf[1], preferred_element_type=F32) + bu_ref[0, 0], -SWIGLU_LIMIT, SWIGLU_LIMIT)
        h = g * jax.nn.sigmoid(SWIGLU_ALPHA * g) * (up + 1.0)
        y = (jnp.dot(h.astype(BF16), wbf[2], preferred_element_type=F32) + bd_ref[0, 0]) * rg_ref[...]

        @pl.when(i >= 1)
        def _():
            scatter.wait(i - 1)

        ybuf[i % 2] = y.reshape(ybuf.shape[1:])


def _moe_experts(layer, blk_e, n_used, tok_tiles, dst_tiles, n_out_rows, x1t, row_gate, w_gate, b_gate, w_up, b_up,
                 w_down, b_down):
    n_blocks = blk_e.shape[0]
    bm = MOE_BLOCK
    tile = x1t.shape[1:]
    d = tile[0] * tile[1]
    ff = w_gate.shape[3]
    assert d == ff

    def wspec(shape):
        return pl.BlockSpec((1, 1) + shape, lambda i, be, nu: (layer, be[i], 0, 0))

    hbm = pl.BlockSpec(memory_space=pl.ANY)
    grid_spec = pltpu.PrefetchScalarGridSpec(
        num_scalar_prefetch=2,
        grid=(n_blocks,),
        in_specs=[hbm, hbm, hbm,
                  pl.BlockSpec((bm, 1), lambda i, be, nu: (i, 0)),
                  wspec((d, ff)), wspec((1, ff)), wspec((d, ff)), wspec((1, ff)), wspec((ff, d)), wspec((1, d))],
        out_specs=hbm,
        scratch_shapes=[pltpu.VMEM((2, bm) + tile, F32), pltpu.VMEM((2, bm) + tile, F32),
                        pltpu.VMEM((3, d, ff), BF16),
                        pltpu.SMEM((2, SUBLANES, LANES), I32), pltpu.SMEM((2, SUBLANES, LANES), I32),
                        pltpu.SemaphoreType.DMA((2,)), pltpu.SemaphoreType.DMA((2,)),
                        pltpu.SemaphoreType.DMA((2,)), pltpu.SemaphoreType.DMA((2,))],
    )
    return pl.pallas_call(
        functools.partial(_moe_kernel, layer),
        grid_spec=grid_spec,
        out_shape=jax.ShapeDtypeStruct((n_out_rows,) + tile, F32),
        compiler_params=_cparams("arbitrary"),
        name="moe_experts",
    )(blk_e, n_used, tok_tiles, dst_tiles, x1t, row_gate, w_gate, b_gate, w_up, b_up, w_down, b_down)


def _combine_kernel(alpha, x_ref, y_ref, lg_ref, lb_ref, o_ref):
    h = y_ref[:, 0]
    for k in range(1, TOP_K):
        h = h + y_ref[:, k]
    o_ref[...] = _layer_norm(alpha * x_ref[...] + h.reshape(x_ref.shape), lg_ref[...], lb_ref[...])


def _combine(alpha, x1, y_tk, ln_g, ln_b):
    t, d = x1.shape
    tm = ROW_TILE
    return pl.pallas_call(
        functools.partial(_combine_kernel, alpha),
        grid=(t // tm,),
        in_specs=[pl.BlockSpec((tm, d), lambda i: (i, 0)),
                  pl.BlockSpec((tm,) + y_tk.shape[1:], lambda i: (i, 0, 0, 0)),
                  pl.BlockSpec(ln_g.shape, lambda i: (0, 0)),
                  pl.BlockSpec(ln_b.shape, lambda i: (0, 0))],
        out_specs=pl.BlockSpec((tm, d), lambda i: (i, 0)),
        out_shape=jax.ShapeDtypeStruct((t, d), F32),
        compiler_params=_cparams("parallel"),
        name="combine_ln",
    )(x1, y_tk, ln_g, ln_b)


def _moe(alpha, layer, x1, x1t, top_e, top_g, w_gate, b_gate, w_up, b_up, w_down, b_down, ln_g, ln_b):
    t, d = x1.shape
    bm = MOE_BLOCK
    n_assign = t * TOP_K
    assert n_assign % bm == 0
    n_blocks = n_assign // bm + N_EXPERTS
    n_rows = n_blocks * bm
    flat_e = top_e.T.reshape(-1)
    experts = jnp.arange(N_EXPERTS, dtype=I32)
    counts = jnp.sum((flat_e[:, None] == experts[None, :]).astype(I32), axis=0)
    padded = (counts + bm - 1) // bm * bm
    pad_end = jnp.cumsum(padded)
    pad_key = jnp.where(jnp.arange(bm, dtype=I32)[None, :] < (padded - counts)[:, None], experts[:, None], N_EXPERTS)
    keys = jnp.concatenate([flat_e, pad_key.reshape(-1)])
    src = jnp.concatenate([jnp.arange(n_assign, dtype=I32), jnp.full((N_EXPERTS * bm,), n_assign, I32)])
    gates = jnp.concatenate([top_g.T.reshape(-1), jnp.zeros((N_EXPERTS * bm,), F32)])
    _, row_src, row_gate = lax.sort((keys, src, gates), num_keys=1, is_stable=True)
    row_token = jnp.where(row_src < n_assign, row_src // TOP_K, 0)
    row_id = jnp.arange(n_rows, dtype=I32)
    spare = n_assign + ((row_id // bm + 1) % 2) * bm + row_id % bm
    row_dst = jnp.where(row_src < n_assign, row_src, spare)
    spare_tile = n_assign + jnp.arange(bm, dtype=I32)
    n_out_rows = n_assign + 2 * bm
    blk_start = jnp.arange(n_blocks, dtype=I32) * bm
    blk_e = jnp.minimum(jnp.sum((pad_end[None, :] <= blk_start[:, None]).astype(I32), axis=1), N_EXPERTS - 1)
    n_used = (pad_end[-1:] // bm).astype(I32)
    row_gate = row_gate.reshape(n_rows, 1)

    per_tile = SUBLANES * LANES

    def tiles(a):
        return jnp.pad(a, ((0, 2), (0, per_tile - a.shape[1]))).reshape(-1, LANES)

    tok_tiles = tiles(row_token.reshape(n_blocks, bm))
    dst_tiles = tiles(jnp.concatenate([spare_tile[None, :], row_dst.reshape(n_blocks, bm)], axis=0))
    y_rows = _moe_experts(layer, blk_e, n_used, tok_tiles, dst_tiles, n_out_rows, x1t, row_gate,
                          w_gate, b_gate, w_up, b_up, w_down, b_down)
    y_tk = y_rows.reshape((n_out_rows // TOP_K, TOP_K) + y_rows.shape[1:])
    return _combine(alpha, x1, y_tk, ln_g, ln_b)


def _t5_bucket(dist):
    exact = NUM_BUCKETS // 2
    d = jnp.maximum(dist, 1).astype(F32)
    large = exact + (jnp.log(d / exact) / math.log(MAX_DISTANCE / exact) * (NUM_BUCKETS - exact)).astype(I32)
    return jnp.where(dist < exact, dist, jnp.minimum(large, NUM_BUCKETS - 1))


def _toeplitz(v, n):
    length = 2 * n - 1
    hankel = jnp.tile(v, (n + 1, 1))[: n * (length + 1)].reshape(n, length + 1, v.shape[1])[:, :n, :]
    return hankel[::-1].transpose(2, 0, 1)


def _prompt_bias_tables(rel_bias, tile):
    far = rel_bias[NUM_BUCKETS - 1]
    by_dist = (rel_bias[_t5_bucket(jnp.arange(2 * tile, dtype=I32))] - far[None, :]) * LOG2_E
    v0 = jnp.concatenate([jnp.full((tile - 1, by_dist.shape[1]), NEG_INF, F32), by_dist[:tile]], axis=0)
    return jnp.stack([_toeplitz(v0, tile), _toeplitz(by_dist[1:], tile)]).astype(F32)


def kernel(x_prompt, x_sample, cache_k, cache_v, cache_kidx, state_conv, state_pool, page_table, rel_bias,
           w_in, b_in, conv_w, pool_w, pool_scale, p_a, p_b, p_c, w_o, ln1_g, ln1_b,
           w_router, b_router, w_gate, b_gate, w_up, b_up, w_down, b_down, ln2_g, ln2_b):
    n_b, seq, d = x_prompt.shape
    n_s, dec_seq, _ = x_sample.shape
    assert dec_seq == 1
    depth = w_in.shape[0]
    n_pages = page_table.shape[1]
    past = n_pages * PAGE_SIZE
    cc = conv_w.shape[2]
    pw = pool_scale.shape[1]
    aw = N_HEADS * HEAD_DIM
    iw = N_IDX_HEADS * IDX_DIM
    assert cc == pw == aw == iw and seq % ATT_TILE == 0 and d % cc == 0
    alpha = float((2 * depth) ** 0.25)
    n_sel_p = min(TOPK_MAX, seq // 4)
    n_sel_s = min(TOPK_MAX, (past + dec_seq) // 4)

    t_p = n_b * seq
    t_all = _round_up(t_p + n_s, ROW_TILE)
    x = jnp.concatenate([x_prompt.reshape(t_p, d), x_sample.reshape(n_s, d),
                         jnp.zeros((t_all - t_p - n_s, d), F32)], axis=0)

    sizes = (cc, cc, cc, pw, aw, aw, aw, iw, IDX_DIM, N_IDX_HEADS, 3 * d)
    n_in = sum(sizes)
    off_g = n_in - 3 * d
    nz = _round_up(n_in, 1536)
    c_ab = 3 * d
    c_q = c_ab + 4 * cc
    c_ki = c_q + 4 * cc
    bias_tab = _prompt_bias_tables(rel_bias, ATT_TILE)
    pt = page_table.astype(I32)
    key_pos = jnp.arange((n_pages + 1) * PAGE_SIZE, dtype=I32)
    bias_tab_s = rel_bias[_t5_bucket(jnp.clip(past - key_pos, 0, MAX_DISTANCE))]
    bias_tab_s = bias_tab_s.reshape(n_pages + 1, PAGE_SIZE, N_HEADS).transpose(0, 2, 1)
    cache_kidx_t = cache_kidx.transpose(0, 1, 3, 2)
    cache_k_t = cache_k.transpose(0, 1, 3, 4, 2)
    cache_v_t = cache_v.transpose(0, 1, 3, 4, 2)

    outs = {name: [] for name in ("kp", "vp", "kip", "cp", "pp", "ks", "vs", "kis", "cs", "ps")}
    for l in range(depth):
        w_l = jnp.concatenate([w_in[l][:, off_g:], w_in[l][:, :off_g], jnp.zeros((d, nz - n_in), F32)], axis=1)
        b_l = jnp.concatenate([b_in[l][off_g:], b_in[l][:off_g], jnp.zeros((nz - n_in,), F32)])[None, :]
        z = _inproj(x, w_l.astype(BF16), b_l)

        def col(c0, width, rows=slice(None)):
            return z[rows, c0:c0 + width]

        sr = slice(t_p, t_p + n_s)
        conv_w_l = conv_w[l]
        pool_w_l = pool_w[l].astype(BF16)
        pool_s_l = pool_scale[l][None, :]

        ya_p, yb_p = _mixer_prompt(z, c_ab, n_b, seq, cc, conv_w_l, pool_w_l, pool_s_l)

        a0 = sum(sizes[:4])
        a1 = a0 + 4 * aw + IDX_DIM + N_IDX_HEADS
        n_att = _round_up(a1 - a0, LANES)
        w_att = jnp.pad(w_in[l][:, a0:a1].T, ((0, n_att - (a1 - a0)), (0, 0))).astype(BF16)
        b_att = jnp.pad(b_in[l][a0:a1], (0, n_att - (a1 - a0)))[:, None]
        q_t, k_t, v_t, v_aug, qi_t, ki_t, w_t, k_bf, ki_bf = _inproj_t(
            x, w_att, b_att, n_b, seq, HEAD_DIM ** -0.5 * LOG2_E, IDX_DIM ** -0.5, N_IDX_HEADS ** -0.5)
        yc_p = _dsa_prompt(qi_t, w_t, ki_bf.reshape(n_b, seq, IDX_DIM), q_t, k_bf.reshape(n_b, seq, aw), v_aug,
                           bias_tab, n_sel_p)

        ya_s, yb_s, u_s = _mixer_sample(
            col(c_ab, cc, sr), col(c_ab + cc, cc, sr), col(c_ab + 2 * cc, cc, sr), col(c_ab + 3 * cc, pw, sr),
            state_conv[l].transpose(1, 0, 2), state_pool[l].transpose(1, 0, 2), past, conv_w_l, pool_w_l, pool_s_l)

        q_s = col(c_q, aw, sr).reshape(n_s, N_HEADS, HEAD_DIM)
        k_s = col(c_q + aw, aw, sr).reshape(n_s, N_HEADS, HEAD_DIM)
        v_s = col(c_q + 2 * aw, aw, sr).reshape(n_s, N_HEADS, HEAD_DIM)
        qi_s = (col(c_q + 3 * aw, iw, sr) * (IDX_DIM ** -0.5)).astype(BF16).reshape(n_s, N_IDX_HEADS, IDX_DIM)
        ki_s = col(c_ki, IDX_DIM, sr)
        wi_s = (col(c_ki + IDX_DIM, N_IDX_HEADS, sr) * (N_IDX_HEADS ** -0.5))[:, :, None]
        ki_new_pad = jnp.pad(ki_s[:, :, None], ((0, 0), (0, 0), (0, PAGE_SIZE - 1)))
        scores = _sample_scores(l, pt, qi_s, wi_s, ki_new_pad, cache_kidx_t)
        mask_add = _sample_select(scores, n_sel_s)
        yc_s = _sample_attend(l, pt, q_s * (HEAD_DIM ** -0.5), mask_add, bias_tab_s, k_s, v_s,
                              cache_k_t, cache_v_t).reshape(n_s, aw)

        pad_rows = t_all - t_p - n_s

        def all_rows(yp, ys_):
            return jnp.concatenate([yp, ys_, jnp.zeros((pad_rows, yp.shape[1]), BF16)], axis=0)

        x1, x1t, top_e, top_g = _merge(
            alpha, x, z, all_rows(ya_p, ya_s), all_rows(yb_p, yb_s),
            all_rows(yc_p.reshape(t_p, aw), yc_s.astype(BF16)),
            p_a[l].astype(BF16), p_b[l].astype(BF16), p_c[l].astype(BF16), w_o[l].astype(BF16),
            ln1_g[l][None, :], ln1_b[l][None, :], w_router[l].T.astype(BF16), b_router[l][:, None])

        x = _moe(alpha, l, x1, x1t, top_e, top_g, w_gate, b_gate[:, :, None, :], w_up, b_up[:, :, None, :],
                 w_down, b_down[:, :, None, :], ln2_g[l][None, :], ln2_b[l][None, :])

        tail = jnp.stack([z[(b + 1) * seq - POOL_STATE:(b + 1) * seq, c_ab + cc:c_ab + 4 * cc] for b in range(n_b)])
        conv_tail = tail[:, POOL_STATE - (CONV_WIDTH - 1):]
        outs["kp"].append(k_t.reshape(n_b, N_HEADS, HEAD_DIM, seq).transpose(0, 3, 1, 2))
        outs["vp"].append(v_t.reshape(n_b, N_HEADS, HEAD_DIM, seq).transpose(0, 3, 1, 2))
        outs["kip"].append(ki_t.transpose(0, 2, 1))
        outs["cp"].append(conv_tail[:, :, :cc] * conv_tail[:, :, cc:2 * cc])
        outs["pp"].append(tail[:, :, 2 * cc:2 * cc + pw])
        outs["ks"].append(k_s[:, None])
        outs["vs"].append(v_s[:, None])
        outs["kis"].append(ki_s[:, None])
        outs["cs"].append(jnp.concatenate([state_conv[l], u_s[:, None]], axis=1)[:, -(CONV_WIDTH - 1):])
        outs["ps"].append(jnp.concatenate([state_pool[l], col(c_ab + 3 * cc, pw, sr)[:, None]], axis=1)[:, -POOL_STATE:])

    y_prompt = x[:t_p].reshape(n_b, seq, d)
    y_sample = x[t_p:t_p + n_s].reshape(n_s, dec_seq, d)
    stk = {k: jnp.stack(v) for k, v in outs.items()}
    return (y_prompt, y_sample, stk["kp"], stk["vp"], stk["kip"], stk["cp"], stk["pp"],
            stk["ks"], stk["vs"], stk["kis"], stk["cs"], stk["ps"])
```

```python
import functools
import math

import numpy as np
import jax
import jax.numpy as jnp
from jax import lax
from jax.experimental import pallas as pl
from jax.experimental.pallas import tpu as pltpu

F32 = jnp.float32
BF16 = jnp.bfloat16
I32 = jnp.int32

N_HEADS = 8
HEAD_DIM = 64
N_IDX_HEADS = 8
IDX_DIM = 64
TOPK_MAX = 256
POOL_WINDOWS = (2, 4, 8, 16)
POOL_STATE = max(POOL_WINDOWS) - 1
CONV_WIDTH = 3
N_EXPERTS = 32
TOP_K = 4
SWIGLU_LIMIT = 7.0
SWIGLU_ALPHA = 1.702
LN_EPS = 1e-5
NUM_BUCKETS = 32
MAX_DISTANCE = 128
PAGE_SIZE = 128

LANES = 128
SUBLANES = 8
VMEM_LIMIT = 56 * 1024 * 1024

ROW_TILE = 256
ATT_TILE = 256
ATT_PIECES = 2
COUNT_UNROLL = 4
VA_ONES = 16
MOE_BLOCK = 256
HALO = 16
INT_MIN = np.int32(-2 ** 31)
LOG2_E = math.log2(math.e)
NEG_INF = float("-inf")


def _cparams(*sem):
    return pltpu.CompilerParams(dimension_semantics=sem, vmem_limit_bytes=VMEM_LIMIT)


def _round_up(n, m):
    return (n + m - 1) // m * m


def _pick_tile(n, candidates):
    for c in candidates:
        if n % c == 0:
            return c
    raise ValueError(f"no tile for {n} in {candidates}")


def _single(block_shape, index_map):
    return pl.BlockSpec(block_shape, index_map, pipeline_mode=pl.Buffered(1))


def _inproj_kernel(x_ref, w_ref, b_ref, o_ref):
    x = x_ref[...].astype(BF16)
    o_ref[...] = jnp.dot(x, w_ref[...], preferred_element_type=F32) + b_ref[...]


def _inproj(x, w, b):
    m, k = x.shape
    n = w.shape[1]
    tm = _pick_tile(m, (1280, 1024, 768, 512, 256))
    tn = _pick_tile(n, (1536, 1280, 1024, 768, 512, 256, 128))
    return pl.pallas_call(
        _inproj_kernel,
        grid=(n // tn, m // tm),
        in_specs=[
            pl.BlockSpec((tm, k), lambda j, i: (i, 0)),
            pl.BlockSpec((k, tn), lambda j, i: (0, j)),
            pl.BlockSpec((1, tn), lambda j, i: (0, j)),
        ],
        out_specs=pl.BlockSpec((tm, tn), lambda j, i: (i, j)),
        out_shape=jax.ShapeDtypeStruct((m, n), F32),
        compiler_params=_cparams("parallel", "parallel"),
        name="inproj",
    )(x, w, b)


def _inproj_t_kernel(scale_q, scale_qi, scale_w, x_ref, w_ref, b_ref, q_ref, k_ref, v_ref, va_ref, qi_ref, ki_ref,
                     wi_ref, kb_ref, kib_ref):
    x = x_ref[...].astype(BF16)
    aw = q_ref.shape[0]
    dims = (((1,), (1,)), ((), ()))

    def proj(r0, n):
        return lax.dot_general(w_ref[r0:r0 + n, :], x, dims, preferred_element_type=F32) + b_ref[r0:r0 + n, :]

    q_ref[...] = (proj(0, aw) * scale_q).astype(BF16)
    k = proj(aw, aw)
    k_ref[...] = k
    kb_ref[...] = k.T.astype(BF16)
    v = proj(2 * aw, aw)
    v_ref[...] = v
    ones = jnp.ones((va_ref.shape[1] - HEAD_DIM, va_ref.shape[2]), BF16)
    for h in range(N_HEADS):
        va_ref[h, 0:HEAD_DIM, :] = v[h * HEAD_DIM:(h + 1) * HEAD_DIM, :].astype(BF16)
        va_ref[h, HEAD_DIM:, :] = ones
    qi_ref[...] = (proj(3 * aw, aw) * scale_qi).astype(BF16)
    tail = proj(4 * aw, w_ref.shape[0] - 4 * aw)
    ki = tail[0:IDX_DIM, :]
    ki_ref[...] = ki
    kib_ref[...] = ki.T.astype(BF16)
    wi_ref[...] = tail[IDX_DIM:IDX_DIM + N_IDX_HEADS, :] * scale_w


def _inproj_t(x, w_t, b_t, n_b, seq, scale_q, scale_qi, scale_w):
    k = x.shape[1]
    aw = N_HEADS * HEAD_DIM
    tm = _pick_tile(seq, (1024, 512, 256))
    spb = seq // tm
    n_tok = n_b * seq
    va_rows = HEAD_DIM + VA_ONES

    def cols(rows):
        return pl.BlockSpec((rows, tm), lambda i: (0, i))

    def seq_cols(rows):
        return pl.BlockSpec((None, rows, tm), lambda i: (i // spb, 0, i % spb))

    def tok_rows(width):
        return pl.BlockSpec((tm, width), lambda i: (i, 0))

    return pl.pallas_call(
        functools.partial(_inproj_t_kernel, scale_q, scale_qi, scale_w),
        grid=(n_tok // tm,),
        in_specs=[pl.BlockSpec((tm, k), lambda i: (i, 0)),
                  pl.BlockSpec(w_t.shape, lambda i: (0, 0)),
                  pl.BlockSpec(b_t.shape, lambda i: (0, 0))],
        out_specs=[cols(aw), seq_cols(aw), seq_cols(aw), pl.BlockSpec((N_HEADS, va_rows, tm), lambda i: (0, 0, i)),
                   cols(aw), seq_cols(IDX_DIM), cols(N_IDX_HEADS), tok_rows(aw), tok_rows(IDX_DIM)],
        out_shape=[jax.ShapeDtypeStruct((aw, n_tok), BF16), jax.ShapeDtypeStruct((n_b, aw, seq), F32),
                   jax.ShapeDtypeStruct((n_b, aw, seq), F32), jax.ShapeDtypeStruct((N_HEADS, va_rows, n_tok), BF16),
                   jax.ShapeDtypeStruct((aw, n_tok), BF16), jax.ShapeDtypeStruct((n_b, IDX_DIM, seq), F32),
                   jax.ShapeDtypeStruct((N_IDX_HEADS, n_tok), F32), jax.ShapeDtypeStruct((n_tok, aw), BF16),
                   jax.ShapeDtypeStruct((n_tok, IDX_DIM), BF16)],
        compiler_params=_cparams("parallel"),
        name="inproj_t",
    )(x, w_t, b_t)


def _pool_out(diffs, pw_ref, ps_ref):
    outs = [jnp.dot(d.astype(BF16), pw_ref[g], preferred_element_type=F32) for g, d in enumerate(diffs)]
    return jnp.concatenate(outs, axis=-1) * ps_ref[...]


def _mixer_prompt_kernel(ab_ref, ac_ref, ah_ref, up_ref, cw_ref, pw_ref, ps_ref, ya_ref, yb_ref, eu_ref, ep_ref):
    si = pl.program_id(1)
    ts = ab_ref.shape[0]
    pg = up_ref.shape[1] // len(POOL_WINDOWS)

    @pl.when(si == 0)
    def _():
        eu_ref[0:HALO, :] = jnp.zeros((HALO, eu_ref.shape[1]), F32)
        ep_ref[0:HALO, :] = jnp.zeros((HALO, ep_ref.shape[1]), F32)

    @pl.when(si > 0)
    def _():
        eu_ref[0:HALO, :] = eu_ref[ts:ts + HALO, :]
        ep_ref[0:HALO, :] = ep_ref[ts:ts + HALO, :]

    u = ac_ref[...] * ah_ref[...]
    eu_ref[HALO:HALO + ts, :] = u
    ep_ref[HALO:HALO + ts, :] = up_ref[...]
    conv = cw_ref[0:1, :] * eu_ref[HALO - 2:HALO - 2 + ts, :]
    conv = conv + cw_ref[1:2, :] * eu_ref[HALO - 1:HALO - 1 + ts, :]
    conv = conv + cw_ref[2:3, :] * u
    ya_ref[...] = (ab_ref[...] * conv).astype(ya_ref.dtype)

    pos = si * ts + lax.broadcasted_iota(I32, (ts, 1), 0)
    diffs = []
    for g, w in enumerate(POOL_WINDOWS):
        sl = slice(g * pg, (g + 1) * pg)
        win = ep_ref[HALO:HALO + ts, sl]
        for j in range(1, w):
            win = win + ep_ref[HALO - j:HALO - j + ts, sl]
        cnt = jnp.minimum(w, pos + 1).astype(F32)
        diffs.append(win / cnt - up_ref[:, sl])
    yb_ref[...] = _pool_out(diffs, pw_ref, ps_ref).astype(yb_ref.dtype)


def _mixer_prompt(z, col0, n_b, seq, cc, conv_w, pool_w_bf, pool_scale):
    ts = _pick_tile(seq, (512, 256, 128))
    nblk = seq // ts
    cb = col0 // cc

    def zspec(i):
        return pl.BlockSpec((ts, cc), lambda b, s, i=i: (b * nblk + s, cb + i))

    out_spec = pl.BlockSpec((ts, cc), lambda b, s: (b * nblk + s, 0))
    return pl.pallas_call(
        _mixer_prompt_kernel,
        grid=(n_b, nblk),
        in_specs=[zspec(0), zspec(1), zspec(2), zspec(3),
                  pl.BlockSpec(conv_w.shape, lambda b, s: (0, 0)),
                  pl.BlockSpec(pool_w_bf.shape, lambda b, s: (0, 0, 0)),
                  pl.BlockSpec(pool_scale.shape, lambda b, s: (0, 0))],
        out_specs=[out_spec, out_spec],
        out_shape=[jax.ShapeDtypeStruct((n_b * seq, cc), BF16)] * 2,
        scratch_shapes=[pltpu.VMEM((HALO + ts, cc), F32), pltpu.VMEM((HALO + ts, cc), F32)],
        compiler_params=_cparams("arbitrary", "arbitrary"),
        name="mixer_prompt",
    )(z, z, z, z, conv_w, pool_w_bf, pool_scale)


def _mixer_sample_kernel(pos0, ab_ref, ac_ref, ah_ref, up_ref, cprev_ref, pprev_ref, cw_ref, pw_ref, ps_ref,
                         ya_ref, yb_ref, u_ref):
    pg = up_ref.shape[1] // len(POOL_WINDOWS)
    u = ac_ref[...] * ah_ref[...]
    u_ref[...] = u
    conv = cw_ref[0:1, :] * cprev_ref[0] + cw_ref[1:2, :] * cprev_ref[1] + cw_ref[2:3, :] * u
    ya_ref[...] = (ab_ref[...] * conv).astype(ya_ref.dtype)
    diffs = []
    for g, w in enumerate(POOL_WINDOWS):
        sl = slice(g * pg, (g + 1) * pg)
        win = up_ref[:, sl]
        for j in range(1, w):
            win = win + pprev_ref[POOL_STATE - j, :, sl]
        cnt = float(min(w, pos0 + 1))
        diffs.append(win / cnt - up_ref[:, sl])
    yb_ref[...] = _pool_out(diffs, pw_ref, ps_ref).astype(yb_ref.dtype)


def _mixer_sample(a_b, a_c, a_h, u_pool, conv_prev_t, pool_prev_t, pos0, conv_w, pool_w_bf, pool_scale):
    n, cc = a_b.shape
    return pl.pallas_call(
        functools.partial(_mixer_sample_kernel, pos0),
        out_shape=[jax.ShapeDtypeStruct((n, cc), BF16), jax.ShapeDtypeStruct((n, cc), BF16),
                   jax.ShapeDtypeStruct((n, cc), F32)],
        compiler_params=pltpu.CompilerParams(vmem_limit_bytes=VMEM_LIMIT),
        name="mixer_sample",
    )(a_b, a_c, a_h, u_pool, conv_prev_t, pool_prev_t, conv_w, pool_w_bf, pool_scale)


def _sortable_key(score):
    bits = lax.bitcast_convert_type(score, I32)
    return bits ^ ((bits >> 31) & np.int32(0x7FFFFFFF))


def _kth_largest_key(count_ge, k, n_total, shape):
    def cond(st):
        it, _, cnt_cur = st
        return (it < 32) & jnp.any(cnt_cur != k)

    def body(st):
        it, thr_u, cnt_cur = st
        cand_u = thr_u | jnp.left_shift(jnp.int32(1), 31 - it)
        cnt = count_ge(cand_u ^ INT_MIN)
        ok = cnt >= k
        return it + 1, jnp.where(ok, cand_u, thr_u), jnp.where(ok, cnt, cnt_cur)

    init = (jnp.int32(0), jnp.zeros(shape, I32), jnp.zeros(shape, I32) + n_total)
    _, thr_u, cnt_cur = lax.while_loop(cond, body, init)
    return thr_u ^ INT_MIN, cnt_cur


def _dsa_prompt_kernel(n_sel, qi_t_ref, w_t_ref, kib_ref, q_t_ref, k_ref, va_ref, bias_ref, o_ref,
                       key_scr, mask_scr, qz_scr, acc_scr, m_scr, alpha_scr, lg_scr, p_scr):
    qi = pl.program_id(1)
    tq = o_ref.shape[0]
    tk = tq
    n_heads = N_HEADS
    dh = HEAD_DIM

    def score_chunk(c):
        r0 = pl.multiple_of(c * tk, tk)
        kc = kib_ref[pl.ds(r0, tk), :]
        acc = jnp.zeros((tk, tq), F32)
        for h in range(N_IDX_HEADS):
            d = jnp.dot(kc, qi_t_ref[h * IDX_DIM:(h + 1) * IDX_DIM, :], preferred_element_type=F32)
            acc = acc + w_t_ref[h:h + 1, :] * jnp.maximum(d, 0.0)
        return acc

    def score_body(c, carry):
        key_scr[c] = _sortable_key(score_chunk(c))
        return carry

    lax.fori_loop(0, qi, score_body, 0)
    key_pos = lax.broadcasted_iota(I32, (tk, tq), 0)
    qry_pos = lax.broadcasted_iota(I32, (tk, tq), 1)
    diag = jnp.where(key_pos <= qry_pos, score_chunk(qi), NEG_INF)
    key_scr[qi] = _sortable_key(diag)

    def count_where(pred):
        def chunk_count(c):
            ind = jnp.where(pred(key_scr[c], c), 1, 0).astype(I32)
            return ind.reshape(tk // SUBLANES, SUBLANES, tq).sum(axis=0)

        def group_body(g, accs):
            return tuple(acc + chunk_count(COUNT_UNROLL * g + u) for u, acc in enumerate(accs))

        n_groups = (qi + 1) // COUNT_UNROLL
        zero = jnp.zeros((SUBLANES, tq), I32)
        accs = lax.fori_loop(0, n_groups, group_body, (zero,) * COUNT_UNROLL)
        acc = functools.reduce(lambda a, b: a + b, accs)
        acc = lax.fori_loop(n_groups * COUNT_UNROLL, qi + 1, lambda c, a: a + chunk_count(c), acc)
        return acc.sum(axis=0, keepdims=True)

    thr, cnt_ge = _kth_largest_key(lambda cand: count_where(lambda blk, c: blk >= cand), n_sel,
                                   (qi + 1) * tk, (1, tq))

    neg_inf_key = _sortable_key(jnp.full((1, tq), NEG_INF, F32))
    overflow = (cnt_ge > n_sel) & (thr > neg_inf_key)

    @pl.when(jnp.any(overflow))
    def _():
        need = n_sel - count_where(lambda blk, c: blk > thr)

        def count_tied_below(bound):
            return count_where(lambda blk, c: (blk == thr) & (c * tk + key_pos < bound))

        def bis_body(it, lohi):
            lo, hi = lohi
            mid = (lo + hi) // 2
            ok = count_tied_below(mid) >= need
            return jnp.where(ok, lo, mid), jnp.where(ok, mid, hi)

        n_keys = (qi + 1) * tk
        lo0 = jnp.zeros((1, tq), I32)
        hi0 = jnp.zeros((1, tq), I32) + n_keys
        n_it = int(math.ceil(math.log2(key_scr.shape[0] * tk))) + 1
        _, bound = lax.fori_loop(0, n_it, bis_body, (lo0, hi0))

        def demote(c, carry):
            blk = key_scr[c]
            drop = overflow & (blk == thr) & (c * tk + key_pos >= bound)
            key_scr[c] = jnp.where(drop, thr - 1, blk)
            return carry

        lax.fori_loop(0, qi + 1, demote, 0)

    zero_half = jnp.zeros((dh, tq), BF16)
    for h in range(n_heads):
        qh = q_t_ref[h * dh:(h + 1) * dh, :]
        pair = (qh, zero_half) if h % 2 == 0 else (zero_half, qh)
        qz_scr[h] = jnp.concatenate(pair, axis=0)
    pk = tk // ATT_PIECES
    m_scr[...] = jnp.full(m_scr.shape, -1e30, F32)
    acc_scr[...] = jnp.zeros(acc_scr.shape, F32)

    def attend(c, bias_idx):
        c0 = pl.multiple_of(c * tk, tk)
        mask_scr[...] = jnp.where(key_scr[c] >= thr, 0.0, NEG_INF).astype(F32)

        def logits_piece(h, j):
            p2 = h // 2
            rows = slice(j * pk, (j + 1) * pk)
            kc = k_ref[pl.ds(pl.multiple_of(c0 + j * pk, pk), pk), p2 * 2 * dh:(p2 + 1) * 2 * dh]
            logit = jnp.dot(kc, qz_scr[h], preferred_element_type=F32) + mask_scr[rows, :]
            if bias_idx is not None:
                logit = logit + bias_ref[bias_idx, h, rows, :]
            lg_scr[h, rows, :] = logit
            return logit.reshape(pk // SUBLANES, SUBLANES, tq).max(axis=0)

        def finish_max(h, cm):
            m_old = m_scr[h]
            m_new = jnp.maximum(m_old, jnp.max(cm, axis=0, keepdims=True))
            m_scr[h] = m_new
            alpha_scr[h] = jnp.exp2(m_old - m_new)

        def exp_piece(h, j):
            rows = slice(j * pk, (j + 1) * pk)
            p_scr[h, rows, :] = jnp.exp2(lg_scr[h, rows, :].reshape(pk // SUBLANES, SUBLANES, tq) - m_scr[h][None]
                                         ).reshape(pk, tq).astype(BF16)

        def value_step(h):
            pv = jnp.dot(va_ref[h, :, pl.ds(c0, tk)], p_scr[h], preferred_element_type=F32)
            acc_scr[h] = alpha_scr[h, 0:1, :] * acc_scr[h] + pv

        for h in range(n_heads):
            cm = None
            for j in range(ATT_PIECES):
                pm = logits_piece(h, j)
                cm = pm if cm is None else jnp.maximum(cm, pm)
            finish_max(h, cm)
        for h in range(n_heads):
            for j in range(ATT_PIECES):
                exp_piece(h, j)
        for h in range(n_heads):
            value_step(h)

    def far_body(c, carry):
        attend(c, None)
        return carry

    lax.fori_loop(0, jnp.maximum(qi - 1, 0), far_body, 0)

    @pl.when(qi >= 1)
    def _():
        attend(qi - 1, 1)

    attend(qi, 0)
    rows = [acc_scr[h, 0:dh, :] / acc_scr[h, dh:dh + 1, :] for h in range(n_heads)]
    o_ref[...] = jnp.concatenate(rows, axis=0).T.astype(o_ref.dtype)


def _dsa_prompt(qi_t, w_t, kib, q_t, k_bf, v_aug, bias_tab, n_sel):
    n_b, seq, width = k_bf.shape
    tq = ATT_TILE
    nq = seq // tq
    va_rows = v_aug.shape[1]
    kernel = functools.partial(_dsa_prompt_kernel, n_sel)
    return pl.pallas_call(
        kernel,
        grid=(n_b, nq),
        in_specs=[
            pl.BlockSpec((qi_t.shape[0], tq), lambda b, q: (0, b * nq + q)),
            pl.BlockSpec((w_t.shape[0], tq), lambda b, q: (0, b * nq + q)),
            _single((None, seq, kib.shape[2]), lambda b, q: (b, 0, 0)),
            pl.BlockSpec((width, tq), lambda b, q: (0, b * nq + q)),
            _single((None, seq, width), lambda b, q: (b, 0, 0)),
            _single((N_HEADS, va_rows, seq), lambda b, q: (0, 0, b)),
            _single(bias_tab.shape, lambda b, q: (0, 0, 0, 0)),
        ],
        out_specs=pl.BlockSpec((None, tq, width), lambda b, q: (b, q, 0)),
        out_shape=jax.ShapeDtypeStruct((n_b, seq, width), BF16),
        scratch_shapes=[
            pltpu.VMEM((nq, tq, tq), I32),
            pltpu.VMEM((tq, tq), F32),
            pltpu.VMEM((N_HEADS, 2 * HEAD_DIM, tq), BF16),
            pltpu.VMEM((N_HEADS, va_rows, tq), F32),
            pltpu.VMEM((N_HEADS, SUBLANES, tq), F32),
            pltpu.VMEM((N_HEADS, SUBLANES, tq), F32),
            pltpu.VMEM((N_HEADS, tq, tq), F32),
            pltpu.VMEM((N_HEADS, tq, tq), BF16),
        ],
        compiler_params=_cparams("arbitrary", "arbitrary"),
        name="dsa_prompt",
    )(qi_t, w_t, kib, q_t, k_bf, v_aug, bias_tab)


def _sample_score_kernel(layer, n_pages, pt_ref, qi_ref, w_ref, knew_ref, kcache_ref, o_ref, kbuf, sem):
    b = pl.program_id(0)
    nb = pl.num_programs(0)
    slot = b % 2

    def page_copy(seq_id, p, s):
        return pltpu.make_async_copy(kcache_ref.at[layer, pt_ref[seq_id, p]], kbuf.at[s, p], sem.at[s])

    def start_all(seq_id, s):
        def body(p, c):
            page_copy(seq_id, p, s).start()
            return c

        lax.fori_loop(0, n_pages, body, 0)

    @pl.when(b == 0)
    def _():
        start_all(0, 0)

    @pl.when(b + 1 < nb)
    def _():
        start_all(b + 1, 1 - slot)

    def wait(p, c):
        page_copy(b, p, slot).wait()
        return c

    lax.fori_loop(0, n_pages, wait, 0)

    qi = qi_ref[0]
    w = w_ref[0]

    def score_rows(keys_t):
        d = jnp.dot(qi, keys_t.astype(BF16), preferred_element_type=F32)
        return jnp.sum(w * jnp.maximum(d, 0.0), axis=0, keepdims=True)

    group = _pick_tile(n_pages, (8, 4, 2, 1))

    def page_body(g, c):
        p0 = pl.multiple_of(g * group, group)
        rows = [score_rows(kbuf[slot, p0 + j]) for j in range(group)]
        o_ref[0, pl.ds(p0, group), :] = jnp.concatenate(rows, axis=0)
        return c

    lax.fori_loop(0, n_pages // group, page_body, 0)
    own = score_rows(knew_ref[0])
    lane = lax.broadcasted_iota(I32, own.shape, 1)
    o_ref[0, n_pages:n_pages + 1, :] = jnp.where(lane == 0, own, NEG_INF)
    n_rows = o_ref.shape[1]
    if n_rows > n_pages + 1:
        o_ref[0, n_pages + 1:, :] = jnp.full((n_rows - n_pages - 1, PAGE_SIZE), NEG_INF, F32)


def _sample_scores(layer, page_table, qi_s, w_s, kidx_new_pad, cache_kidx_t):
    n_b, n_pages = page_table.shape
    n_rows = _round_up(n_pages + 1, SUBLANES)
    grid_spec = pltpu.PrefetchScalarGridSpec(
        num_scalar_prefetch=1,
        grid=(n_b,),
        in_specs=[
            pl.BlockSpec((1, N_IDX_HEADS, IDX_DIM), lambda b, pt: (b, 0, 0)),
            pl.BlockSpec((1, N_IDX_HEADS, 1), lambda b, pt: (b, 0, 0)),
            pl.BlockSpec((1, IDX_DIM, PAGE_SIZE), lambda b, pt: (b, 0, 0)),
            pl.BlockSpec(memory_space=pl.ANY),
        ],
        out_specs=pl.BlockSpec((1, n_rows, PAGE_SIZE), lambda b, pt: (b, 0, 0)),
        scratch_shapes=[pltpu.VMEM((2, n_pages, IDX_DIM, PAGE_SIZE), F32), pltpu.SemaphoreType.DMA((2,))],
    )
    return pl.pallas_call(
        functools.partial(_sample_score_kernel, layer, n_pages),
        grid_spec=grid_spec,
        out_shape=jax.ShapeDtypeStruct((n_b, n_rows, PAGE_SIZE), F32),
        compiler_params=_cparams("arbitrary"),
        name="sample_scores",
    )(page_table, qi_s, w_s, kidx_new_pad, cache_kidx_t)


def _sample_select_kernel(n_sel, s_ref, o_ref):
    score = s_ref[...]
    n_b, n_rows, _ = score.shape
    key = _sortable_key(score)
    pos = lax.broadcasted_iota(I32, key.shape, 1) * PAGE_SIZE + lax.broadcasted_iota(I32, key.shape, 2)

    def count(pred):
        ones = jnp.where(pred, 1, 0).astype(I32)
        return jnp.sum(jnp.sum(ones, axis=1, keepdims=True), axis=2, keepdims=True)

    thr, _ = _kth_largest_key(lambda cand: count(key >= cand), n_sel, n_rows * PAGE_SIZE, (n_b, 1, 1))
    need = n_sel - count(key > thr)

    def bis_body(it, lohi):
        lo, hi = lohi
        mid = (lo + hi) // 2
        ok = count((key == thr) & (pos < mid)) >= need
        return jnp.where(ok, lo, mid), jnp.where(ok, mid, hi)

    n_it = int(math.ceil(math.log2(n_rows * PAGE_SIZE))) + 1
    _, bound = lax.fori_loop(0, n_it, bis_body,
                             (jnp.zeros((n_b, 1, 1), I32), jnp.full((n_b, 1, 1), n_rows * PAGE_SIZE, I32)))
    sel = (key > thr) | ((key == thr) & (pos < bound))
    o_ref[...] = jnp.where(sel, 0.0, NEG_INF).astype(F32)


def _sample_select(scores, n_sel):
    return pl.pallas_call(
        functools.partial(_sample_select_kernel, n_sel),
        out_shape=jax.ShapeDtypeStruct(scores.shape, F32),
        compiler_params=pltpu.CompilerParams(vmem_limit_bytes=VMEM_LIMIT),
        name="sample_select",
    )(scores)


def _sample_attend_kernel(layer, n_pages, pt_ref, q3_ref, q2_ref, mask_ref, bias_ref, knew_ref, vnew_ref,
                          kcache_ref, vcache_ref, o_ref, kbuf, vbuf, qb_scr, p_scr, acc_scr, ksem, vsem):
    b = pl.program_id(0)
    ppc = kbuf.shape[1]
    n_chunks = n_pages // ppc

    def page_copy(cache_ref, buf, sem, c, j, s):
        return pltpu.make_async_copy(cache_ref.at[layer, pt_ref[b, c * ppc + j]], buf.at[s, j], sem.at[s])

    def start(cache_ref, buf, sem, c, s):
        for j in range(ppc):
            page_copy(cache_ref, buf, sem, c, j, s).start()

    def wait(cache_ref, buf, sem, c, s):
        for j in range(ppc):
            page_copy(cache_ref, buf, sem, c, j, s).wait()

    for c0 in range(min(2, n_chunks)):
        start(kcache_ref, kbuf, ksem, c0, c0)
        start(vcache_ref, vbuf, vsem, c0, c0)
    qb_scr[...] = jnp.broadcast_to(q3_ref[0], qb_scr.shape)

    def k_body(c, carry):
        s = c % 2
        wait(kcache_ref, kbuf, ksem, c, s)
        for j in range(ppc):
            page = c * ppc + j
            logit = jnp.sum(kbuf[s, j] * qb_scr[...], axis=1)
            p_scr[page] = logit + bias_ref[page] + mask_ref[0, pl.ds(page, 1), :]

        @pl.when(c + 2 < n_chunks)
        def _():
            start(kcache_ref, kbuf, ksem, c + 2, s)

        return carry

    lax.fori_loop(0, n_chunks, k_body, 0)
    own = jnp.sum(q2_ref[0] * knew_ref[0], axis=1, keepdims=True)
    lane = lax.broadcasted_iota(I32, (N_HEADS, PAGE_SIZE), 1)
    p_scr[n_pages] = jnp.where(lane == 0, own, NEG_INF) + bias_ref[n_pages] + mask_ref[0, n_pages:n_pages + 1, :]

    logits = p_scr[...]
    m = jnp.max(jnp.max(logits, axis=0), axis=1, keepdims=True)
    p = jnp.exp(logits - m[None])
    denom = jnp.sum(jnp.sum(p, axis=0), axis=1, keepdims=True)
    p_scr[...] = p

    acc_scr[...] = jnp.zeros(acc_scr.shape, F32)

    def v_body(c, carry):
        s = c % 2
        wait(vcache_ref, vbuf, vsem, c, s)
        for h in range(N_HEADS):
            acc = acc_scr[h]
            for j in range(ppc):
                acc = acc + p_scr[c * ppc + j, h:h + 1, :] * vbuf[s, j, h]
            acc_scr[h] = acc

        @pl.when(c + 2 < n_chunks)
        def _():
            start(vcache_ref, vbuf, vsem, c + 2, s)

        return carry

    lax.fori_loop(0, n_chunks, v_body, 0)
    out = jnp.sum(acc_scr[...], axis=2) + p_scr[n_pages][:, 0:1] * vnew_ref[0]
    o_ref[0] = out / denom


def _sample_attend(layer, page_table, q_s, mask_add, bias_tab, k_new, v_new, cache_k_t, cache_v_t):
    n_b, n_pages = page_table.shape
    n_rows = mask_add.shape[1]
    ppc = _pick_tile(n_pages, (16, 8, 4, 2, 1))
    slab = (N_HEADS, HEAD_DIM, PAGE_SIZE)
    grid_spec = pltpu.PrefetchScalarGridSpec(
        num_scalar_prefetch=1,
        grid=(n_b,),
        in_specs=[
            pl.BlockSpec((1, N_HEADS, HEAD_DIM, 1), lambda b, pt: (b, 0, 0, 0)),
            pl.BlockSpec((1, N_HEADS, HEAD_DIM), lambda b, pt: (b, 0, 0)),
            pl.BlockSpec((1, n_rows, PAGE_SIZE), lambda b, pt: (b, 0, 0)),
            pl.BlockSpec(bias_tab.shape, lambda b, pt: (0, 0, 0)),
            pl.BlockSpec((1, N_HEADS, HEAD_DIM), lambda b, pt: (b, 0, 0)),
            pl.BlockSpec((1, N_HEADS, HEAD_DIM), lambda b, pt: (b, 0, 0)),
            pl.BlockSpec(memory_space=pl.ANY),
            pl.BlockSpec(memory_space=pl.ANY),
        ],
        out_specs=pl.BlockSpec((1, N_HEADS, HEAD_DIM), lambda b, pt: (b, 0, 0)),
        scratch_shapes=[pltpu.VMEM((2, ppc) + slab, F32), pltpu.VMEM((2, ppc) + slab, F32),
                        pltpu.VMEM(slab, F32), pltpu.VMEM((n_pages + 1, N_HEADS, PAGE_SIZE), F32),
                        pltpu.VMEM(slab, F32), pltpu.SemaphoreType.DMA((2,)), pltpu.SemaphoreType.DMA((2,))],
    )
    return pl.pallas_call(
        functools.partial(_sample_attend_kernel, layer, n_pages),
        grid_spec=grid_spec,
        out_shape=jax.ShapeDtypeStruct((n_b, N_HEADS, HEAD_DIM), F32),
        compiler_params=_cparams("arbitrary"),
        name="sample_attend",
    )(page_table, q_s[..., None], q_s, mask_add, bias_tab, k_new, v_new, cache_k_t, cache_v_t)


def _layer_norm(x, g, b):
    mu = jnp.mean(x, axis=-1, keepdims=True)
    xc = x - mu
    var = jnp.mean(xc * xc, axis=-1, keepdims=True)
    return xc * lax.rsqrt(var + LN_EPS) * g + b


def _merge_kernel(alpha, x_ref, ga_ref, gb_ref, gc_ref, ya_ref, yb_ref, yc_ref, pa_ref, pb_ref, pc_ref, wo_ref,
                  g1_ref, b1_ref, wr_ref, br_ref, x1_ref, x1t_ref, te_ref, tg_ref):
    def branch(g_ref, y_ref, p_ref):
        return jax.nn.sigmoid(g_ref[...]) * jnp.dot(y_ref[...], p_ref[...], preferred_element_type=F32)

    merged = branch(ga_ref, ya_ref, pa_ref) + branch(gb_ref, yb_ref, pb_ref) + branch(gc_ref, yc_ref, pc_ref)
    y = alpha * x_ref[...] + jnp.dot(merged.astype(BF16), wo_ref[...], preferred_element_type=F32)
    x1 = _layer_norm(y, g1_ref[...], b1_ref[...])
    x1_ref[...] = x1
    x1t_ref[...] = x1.reshape(x1t_ref.shape)

    dims = (((1,), (1,)), ((), ()))
    logits = lax.dot_general(wr_ref[...], x1.astype(BF16), dims, preferred_element_type=F32) + br_ref[...]
    eidx = lax.broadcasted_iota(I32, logits.shape, 0)
    vals, idxs = [], []
    cur = logits
    for _ in range(TOP_K):
        v = jnp.max(cur, axis=0, keepdims=True)
        i = jnp.min(jnp.where(cur == v, eidx, N_EXPERTS), axis=0, keepdims=True)
        vals.append(v)
        idxs.append(i)
        cur = jnp.where(eidx == i, NEG_INF, cur)
    top_v = jnp.concatenate(vals, axis=0)
    e = jnp.exp(top_v - vals[0])
    tg_ref[...] = e / jnp.sum(e, axis=0, keepdims=True)
    te_ref[...] = jnp.concatenate(idxs, axis=0)


def _merge(alpha, x, z, y_a, y_b, y_c, p_a, p_b, p_c, w_o, ln_g, ln_b, w_r, b_r):
    t, d = x.shape
    cc = y_a.shape[1]
    tm = ROW_TILE

    def const(a):
        return pl.BlockSpec(a.shape, lambda i: (0,) * a.ndim)

    def rows(width, col=0):
        return pl.BlockSpec((tm, width), lambda i, col=col: (i, col))

    return pl.pallas_call(
        functools.partial(_merge_kernel, alpha),
        grid=(t // tm,),
        in_specs=[rows(d), rows(d, 0), rows(d, 1), rows(d, 2), rows(cc), rows(cc), rows(cc),
                  const(p_a), const(p_b), const(p_c), const(w_o), const(ln_g), const(ln_b),
                  const(w_r), const(b_r)],
        out_specs=[rows(d), pl.BlockSpec((tm, d // LANES, LANES), lambda i: (i, 0, 0)),
                   pl.BlockSpec((TOP_K, tm), lambda i: (0, i)), pl.BlockSpec((TOP_K, tm), lambda i: (0, i))],
        out_shape=[jax.ShapeDtypeStruct((t, d), F32), jax.ShapeDtypeStruct((t, d // LANES, LANES), F32),
                   jax.ShapeDtypeStruct((TOP_K, t), I32), jax.ShapeDtypeStruct((TOP_K, t), F32)],
        compiler_params=_cparams("parallel"),
        name="merge_ln_router",
    )(x, z, z, z, y_a, y_b, y_c, p_a, p_b, p_c, w_o, ln_g, ln_b, w_r, b_r)


class _RowGather:
    def __init__(self, idx_ref, src_ref, buf, idx_smem, isem, dsem, rows):
        self.idx_ref, self.src_ref, self.buf = idx_ref, src_ref, buf
        self.idx_smem, self.isem, self.dsem, self.rows = idx_smem, isem, dsem, rows

    def _idx_copy(self, step):
        s = step % 2
        return pltpu.make_async_copy(self.idx_ref.at[pl.ds(step * SUBLANES, SUBLANES), :], self.idx_smem.at[s],
                                     self.isem.at[s])

    def _row_copy(self, token, s, r):
        return pltpu.make_async_copy(self.src_ref.at[token], self.buf.at[s, r], self.dsem.at[s])

    def issue_rows(self, step):
        s = step % 2
        for r in range(self.rows):
            self._row_copy(self.idx_smem[s, r // LANES, r % LANES], s, r).start()
        self._idx_copy(step + 1).start()

    def begin(self, i):
        @pl.when(i == 0)
        def _():
            self._idx_copy(0).start()
            self._idx_copy(0).wait()
            self.issue_rows(0)

        self._idx_copy(i + 1).wait()

    def wait_rows(self, i):
        s = i % 2
        for r in range(self.rows):
            self._row_copy(0, s, r).wait()

    def drain(self, n):
        self.wait_rows(n)
        self._idx_copy(n + 1).wait()


class _RowScatter:
    def __init__(self, idx_ref, dst_ref, buf, idx_smem, isem, dsem, rows, spare_row0):
        self.idx_ref, self.dst_ref, self.buf = idx_ref, dst_ref, buf
        self.idx_smem, self.isem, self.dsem, self.rows = idx_smem, isem, dsem, rows
        self.spare_row0 = spare_row0

    def _idx_copy(self, step):
        s = step % 2
        return pltpu.make_async_copy(self.idx_ref.at[pl.ds(step * SUBLANES, SUBLANES), :], self.idx_smem.at[s],
                                     self.isem.at[s])

    def _row_copy(self, row_id, s, r):
        return pltpu.make_async_copy(self.buf.at[s, r], self.dst_ref.at[row_id], self.dsem.at[s])

    def begin(self, i):
        @pl.when(i == 0)
        def _():
            self.buf[1] = jnp.zeros(self.buf.shape[1:], self.buf.dtype)
            second = self.dst_ref.at[pl.ds(self.spare_row0 + self.rows, self.rows)]
            fill = pltpu.make_async_copy(self.buf.at[1], second, self.dsem.at[1])
            fill.start()
            fill.wait()
            self._idx_copy(0).start()

    def issue(self, i, prefetch=True):
        self._idx_copy(i).wait()
        s = (i + 1) % 2
        for r in range(self.rows):
            self._row_copy(self.idx_smem[i % 2, r // LANES, r % LANES], s, r).start()
        if prefetch:
            self._idx_copy(i + 1).start()

    def wait(self, i):
        s = (i + 1) % 2
        for r in range(self.rows):
            self._row_copy(0, s, r).wait()

    def drain(self, n):
        self.wait(n - 1)
        self.issue(n, prefetch=False)
        self.wait(n)


def _moe_kernel(layer, blk_e_ref, n_used_ref, idx_ref, dst_ref, x_hbm_ref, rg_ref, wg_ref, bg_ref, wu_ref, bu_ref,
                wd_ref, bd_ref, y_hbm_ref, xbuf, ybuf, wbf, idx_smem, dst_smem, isem, dsem, osem, ssem):
    i = pl.program_id(0)
    n_used = n_used_ref[0]
    bm = xbuf.shape[1]
    gather = _RowGather(idx_ref, x_hbm_ref, xbuf, idx_smem, isem, dsem, bm)
    scatter = _RowScatter(dst_ref, y_hbm_ref, ybuf, dst_smem, osem, ssem, bm, y_hbm_ref.shape[0] - 2 * bm)

    @pl.when(i == n_used)
    def _():
        gather.drain(i)
        scatter.drain(i)

    @pl.when(i < n_used)
    def _():
        gather.begin(i)
        scatter.begin(i)
        gather.wait_rows(i)

        @pl.when((i == 0) | (blk_e_ref[i] != blk_e_ref[jnp.maximum(i - 1, 0)]))
        def _():
            wbf[0] = wg_ref[0, 0].astype(BF16)
            wbf[1] = wu_ref[0, 0].astype(BF16)
            wbf[2] = wd_ref[0, 0].astype(BF16)

        gather.issue_rows(i + 1)
        scatter.issue(i)
        xe = xbuf[i % 2].reshape(bm, wbf.shape[1]).astype(BF16)
        g = jnp.minimum(jnp.dot(xe, wbf[0], preferred_element_type=F32) + bg_ref[0, 0], SWIGLU_LIMIT)
        up = jnp.clip(jnp.dot(xe, wbf[1], preferred_element_type=F32) + bu_ref[0, 0], -SWIGLU_LIMIT, SWIGLU_LIMIT)
        h = g * jax.nn.sigmoid(SWIGLU_ALPHA * g) * (up + 1.0)
        y = (jnp.dot(h.astype(BF16), wbf[2], preferred_element_type=F32) + bd_ref[0, 0]) * rg_ref[...]

        @pl.when(i >= 1)
        def _():
            scatter.wait(i - 1)

        ybuf[i % 2] = y.reshape(ybuf.shape[1:])


def _moe_experts(layer, blk_e, n_used, tok_tiles, dst_tiles, n_out_rows, x1t, row_gate, w_gate, b_gate, w_up, b_up,
                 w_down, b_down):
    n_blocks = blk_e.shape[0]
    bm = MOE_BLOCK
    tile = x1t.shape[1:]
    d = tile[0] * tile[1]
    ff = w_gate.shape[3]
    assert d == ff

    def wspec(shape):
        return pl.BlockSpec((1, 1) + shape, lambda i, be, nu: (layer, be[i], 0, 0))

    hbm = pl.BlockSpec(memory_space=pl.ANY)
    grid_spec = pltpu.PrefetchScalarGridSpec(
        num_scalar_prefetch=2,
        grid=(n_blocks,),
        in_specs=[hbm, hbm, hbm,
                  pl.BlockSpec((bm, 1), lambda i, be, nu: (i, 0)),
                  wspec((d, ff)), wspec((1, ff)), wspec((d, ff)), wspec((1, ff)), wspec((ff, d)), wspec((1, d))],
        out_specs=hbm,
        scratch_shapes=[pltpu.VMEM((2, bm) + tile, F32), pltpu.VMEM((2, bm) + tile, F32),
                        pltpu.VMEM((3, d, ff), BF16),
                        pltpu.SMEM((2, SUBLANES, LANES), I32), pltpu.SMEM((2, SUBLANES, LANES), I32),
                        pltpu.SemaphoreType.DMA((2,)), pltpu.SemaphoreType.DMA((2,)),
                        pltpu.SemaphoreType.DMA((2,)), pltpu.SemaphoreType.DMA((2,))],
    )
    return pl.pallas_call(
        functools.partial(_moe_kernel, layer),
        grid_spec=grid_spec,
        out_shape=jax.ShapeDtypeStruct((n_out_rows,) + tile, F32),
        compiler_params=_cparams("arbitrary"),
        name="moe_experts",
    )(blk_e, n_used, tok_tiles, dst_tiles, x1t, row_gate, w_gate, b_gate, w_up, b_up, w_down, b_down)


def _combine_kernel(alpha, x_ref, y_ref, lg_ref, lb_ref, o_ref):
    h = y_ref[:, 0]
    for k in range(1, TOP_K):
        h = h + y_ref[:, k]
    o_ref[...] = _layer_norm(alpha * x_ref[...] + h.reshape(x_ref.shape), lg_ref[...], lb_ref[...])


def _combine(alpha, x1, y_tk, ln_g, ln_b):
    t, d = x1.shape
    tm = ROW_TILE
    return pl.pallas_call(
        functools.partial(_combine_kernel, alpha),
        grid=(t // tm,),
        in_specs=[pl.BlockSpec((tm, d), lambda i: (i, 0)),
                  pl.BlockSpec((tm,) + y_tk.shape[1:], lambda i: (i, 0, 0, 0)),
                  pl.BlockSpec(ln_g.shape, lambda i: (0, 0)),
                  pl.BlockSpec(ln_b.shape, lambda i: (0, 0))],
        out_specs=pl.BlockSpec((tm, d), lambda i: (i, 0)),
        out_shape=jax.ShapeDtypeStruct((t, d), F32),
        compiler_params=_cparams("parallel"),
        name="combine_ln",
    )(x1, y_tk, ln_g, ln_b)


def _moe(alpha, layer, x1, x1t, top_e, top_g, w_gate, b_gate, w_up, b_up, w_down, b_down, ln_g, ln_b):
    t, d = x1.shape
    bm = MOE_BLOCK
    n_assign = t * TOP_K
    assert n_assign % bm == 0
    n_blocks = n_assign // bm + N_EXPERTS
    n_rows = n_blocks * bm
    flat_e = top_e.T.reshape(-1)
    experts = jnp.arange(N_EXPERTS, dtype=I32)
    counts = jnp.sum((flat_e[:, None] == experts[None, :]).astype(I32), axis=0)
    padded = (counts + bm - 1) // bm * bm
    pad_end = jnp.cumsum(padded)
    pad_key = jnp.where(jnp.arange(bm, dtype=I32)[None, :] < (padded - counts)[:, None], experts[:, None], N_EXPERTS)
    keys = jnp.concatenate([flat_e, pad_key.reshape(-1)])
    src = jnp.concatenate([jnp.arange(n_assign, dtype=I32), jnp.full((N_EXPERTS * bm,), n_assign, I32)])
    gates = jnp.concatenate([top_g.T.reshape(-1), jnp.zeros((N_EXPERTS * bm,), F32)])
    _, row_src, row_gate = lax.sort((keys, src, gates), num_keys=1, is_stable=True)
    row_token = jnp.where(row_src < n_assign, row_src // TOP_K, 0)
    row_id = jnp.arange(n_rows, dtype=I32)
    spare = n_assign + ((row_id // bm + 1) % 2) * bm + row_id % bm
    row_dst = jnp.where(row_src < n_assign, row_src, spare)
    spare_tile = n_assign + jnp.arange(bm, dtype=I32)
    n_out_rows = n_assign + 2 * bm
    blk_start = jnp.arange(n_blocks, dtype=I32) * bm
    blk_e = jnp.minimum(jnp.sum((pad_end[None, :] <= blk_start[:, None]).astype(I32), axis=1), N_EXPERTS - 1)
    n_used = (pad_end[-1:] // bm).astype(I32)
    row_gate = row_gate.reshape(n_rows, 1)

    per_tile = SUBLANES * LANES

    def tiles(a):
        return jnp.pad(a, ((0, 2), (0, per_tile - a.shape[1]))).reshape(-1, LANES)

    tok_tiles = tiles(row_token.reshape(n_blocks, bm))
    dst_tiles = tiles(jnp.concatenate([spare_tile[None, :], row_dst.reshape(n_blocks, bm)], axis=0))
    y_rows = _moe_experts(layer, blk_e, n_used, tok_tiles, dst_tiles, n_out_rows, x1t, row_gate,
                          w_gate, b_gate, w_up, b_up, w_down, b_down)
    y_tk = y_rows.reshape((n_out_rows // TOP_K, TOP_K) + y_rows.shape[1:])
    return _combine(alpha, x1, y_tk, ln_g, ln_b)


def _t5_bucket(dist):
    exact = NUM_BUCKETS // 2
    d = jnp.maximum(dist, 1).astype(F32)
    large = exact + (jnp.log(d / exact) / math.log(MAX_DISTANCE / exact) * (NUM_BUCKETS - exact)).astype(I32)
    return jnp.where(dist < exact, dist, jnp.minimum(large, NUM_BUCKETS - 1))


def _toeplitz(v, n):
    length = 2 * n - 1
    hankel = jnp.tile(v, (n + 1, 1))[: n * (length + 1)].reshape(n, length + 1, v.shape[1])[:, :n, :]
    return hankel[::-1].transpose(2, 0, 1)


def _prompt_bias_tables(rel_bias, tile):
    far = rel_bias[NUM_BUCKETS - 1]
    by_dist = (rel_bias[_t5_bucket(jnp.arange(2 * tile, dtype=I32))] - far[None, :]) * LOG2_E
    v0 = jnp.concatenate([jnp.full((tile - 1, by_dist.shape[1]), NEG_INF, F32), by_dist[:tile]], axis=0)
    return jnp.stack([_toeplitz(v0, tile), _toeplitz(by_dist[1:], tile)]).astype(F32)


def kernel(x_prompt, x_sample, cache_k, cache_v, cache_kidx, state_conv, state_pool, page_table, rel_bias,
           w_in, b_in, conv_w, pool_w, pool_scale, p_a, p_b, p_c, w_o, ln1_g, ln1_b,
           w_router, b_router, w_gate, b_gate, w_up, b_up, w_down, b_down, ln2_g, ln2_b):
    n_b, seq, d = x_prompt.shape
    n_s, dec_seq, _ = x_sample.shape
    assert dec_seq == 1
    depth = w_in.shape[0]
    n_pages = page_table.shape[1]
    past = n_pages * PAGE_SIZE
    cc = conv_w.shape[2]
    pw = pool_scale.shape[1]
    aw = N_HEADS * HEAD_DIM
    iw = N_IDX_HEADS * IDX_DIM
    assert cc == pw == aw == iw and seq % ATT_TILE == 0 and d % cc == 0
    alpha = float((2 * depth) ** 0.25)
    n_sel_p = min(TOPK_MAX, seq // 4)
    n_sel_s = min(TOPK_MAX, (past + dec_seq) // 4)

    t_p = n_b * seq
    t_all = _round_up(t_p + n_s, ROW_TILE)
    x = jnp.concatenate([x_prompt.reshape(t_p, d), x_sample.reshape(n_s, d),
                         jnp.zeros((t_all - t_p - n_s, d), F32)], axis=0)

    sizes = (cc, cc, cc, pw, aw, aw, aw, iw, IDX_DIM, N_IDX_HEADS, 3 * d)
    n_in = sum(sizes)
    off_g = n_in - 3 * d
    nz = _round_up(n_in, 1536)
    c_ab = 3 * d
    c_q = c_ab + 4 * cc
    c_ki = c_q + 4 * cc
    bias_tab = _prompt_bias_tables(rel_bias, ATT_TILE)
    pt = page_table.astype(I32)
    key_pos = jnp.arange((n_pages + 1) * PAGE_SIZE, dtype=I32)
    bias_tab_s = rel_bias[_t5_bucket(jnp.clip(past - key_pos, 0, MAX_DISTANCE))]
    bias_tab_s = bias_tab_s.reshape(n_pages + 1, PAGE_SIZE, N_HEADS).transpose(0, 2, 1)
    cache_kidx_t = cache_kidx.transpose(0, 1, 3, 2)
    cache_k_t = cache_k.transpose(0, 1, 3, 4, 2)
    cache_v_t = cache_v.transpose(0, 1, 3, 4, 2)

    outs = {name: [] for name in ("kp", "vp", "kip", "cp", "pp", "ks", "vs", "kis", "cs", "ps")}
    for l in range(depth):
        w_l = jnp.concatenate([w_in[l][:, off_g:], w_in[l][:, :off_g], jnp.zeros((d, nz - n_in), F32)], axis=1)
        b_l = jnp.concatenate([b_in[l][off_g:], b_in[l][:off_g], jnp.zeros((nz - n_in,), F32)])[None, :]
        w_l = w_l.astype(BF16)
        z = _inproj(x, w_l[:, :c_q], b_l[:, :c_q])
        z_s = _inproj(x[t_p:], w_l[:, c_q:], b_l[:, c_q:])

        def col(c0, width, rows=slice(None)):
            if c0 >= c_q:
                assert rows.start == t_p
                return z_s[0:rows.stop - rows.start, c0 - c_q:c0 - c_q + width]
            return z[rows, c0:c0 + width]

        sr = slice(t_p, t_p + n_s)
        conv_w_l = conv_w[l]
        pool_w_l = pool_w[l].astype(BF16)
        pool_s_l = pool_scale[l][None, :]

        ya_p, yb_p = _mixer_prompt(z, c_ab, n_b, seq, cc, conv_w_l, pool_w_l, pool_s_l)

        a0 = sum(sizes[:4])
        a1 = a0 + 4 * aw + IDX_DIM + N_IDX_HEADS
        n_att = _round_up(a1 - a0, LANES)
        w_att = jnp.pad(w_in[l][:, a0:a1].T, ((0, n_att - (a1 - a0)), (0, 0))).astype(BF16)
        b_att = jnp.pad(b_in[l][a0:a1], (0, n_att - (a1 - a0)))[:, None]
        q_t, k_t, v_t, v_aug, qi_t, ki_t, w_t, k_bf, ki_bf = _inproj_t(
            x, w_att, b_att, n_b, seq, HEAD_DIM ** -0.5 * LOG2_E, IDX_DIM ** -0.5, N_IDX_HEADS ** -0.5)
        yc_p = _dsa_prompt(qi_t, w_t, ki_bf.reshape(n_b, seq, IDX_DIM), q_t, k_bf.reshape(n_b, seq, aw), v_aug,
                           bias_tab, n_sel_p)

        ya_s, yb_s, u_s = _mixer_sample(
            col(c_ab, cc, sr), col(c_ab + cc, cc, sr), col(c_ab + 2 * cc, cc, sr), col(c_ab + 3 * cc, pw, sr),
            state_conv[l].transpose(1, 0, 2), state_pool[l].transpose(1, 0, 2), past, conv_w_l, pool_w_l, pool_s_l)

        q_s = col(c_q, aw, sr).reshape(n_s, N_HEADS, HEAD_DIM)
        k_s = col(c_q + aw, aw, sr).reshape(n_s, N_HEADS, HEAD_DIM)
        v_s = col(c_q + 2 * aw, aw, sr).reshape(n_s, N_HEADS, HEAD_DIM)
        qi_s = (col(c_q + 3 * aw, iw, sr) * (IDX_DIM ** -0.5)).astype(BF16).reshape(n_s, N_IDX_HEADS, IDX_DIM)
        ki_s = col(c_ki, IDX_DIM, sr)
        wi_s = (col(c_ki + IDX_DIM, N_IDX_HEADS, sr) * (N_IDX_HEADS ** -0.5))[:, :, None]
        ki_new_pad = jnp.pad(ki_s[:, :, None], ((0, 0), (0, 0), (0, PAGE_SIZE - 1)))
        scores = _sample_scores(l, pt, qi_s, wi_s, ki_new_pad, cache_kidx_t)
        mask_add = _sample_select(scores, n_sel_s)
        yc_s = _sample_attend(l, pt, q_s * (HEAD_DIM ** -0.5), mask_add, bias_tab_s, k_s, v_s,
                              cache_k_t, cache_v_t).reshape(n_s, aw)

        pad_rows = t_all - t_p - n_s

        def all_rows(yp, ys_):
            return jnp.concatenate([yp, ys_, jnp.zeros((pad_rows, yp.shape[1]), BF16)], axis=0)

        x1, x1t, top_e, top_g = _merge(
            alpha, x, z, all_rows(ya_p, ya_s), all_rows(yb_p, yb_s),
            all_rows(yc_p.reshape(t_p, aw), yc_s.astype(BF16)),
            p_a[l].astype(BF16), p_b[l].astype(BF16), p_c[l].astype(BF16), w_o[l].astype(BF16),
            ln1_g[l][None, :], ln1_b[l][None, :], w_router[l].T.astype(BF16), b_router[l][:, None])

        x = _moe(alpha, l, x1, x1t, top_e, top_g, w_gate, b_gate[:, :, None, :], w_up, b_up[:, :, None, :],
                 w_down, b_down[:, :, None, :], ln2_g[l][None, :], ln2_b[l][None, :])

        tail = jnp.stack([z[(b + 1) * seq - POOL_STATE:(b + 1) * seq, c_ab + cc:c_ab + 4 * cc] for b in range(n_b)])
        conv_tail = tail[:, POOL_STATE - (CONV_WIDTH - 1):]
        outs["kp"].append(k_t.reshape(n_b, N_HEADS, HEAD_DIM, seq).transpose(0, 3, 1, 2))
        outs["vp"].append(v_t.reshape(n_b, N_HEADS, HEAD_DIM, seq).transpose(0, 3, 1, 2))
        outs["kip"].append(ki_t.transpose(0, 2, 1))
        outs["cp"].append(conv_tail[:, :, :cc] * conv_tail[:, :, cc:2 * cc])
        outs["pp"].append(tail[:, :, 2 * cc:2 * cc + pw])
        outs["ks"].append(k_s[:, None])
        outs["vs"].append(v_s[:, None])
        outs["kis"].append(ki_s[:, None])
        outs["cs"].append(jnp.concatenate([state_conv[l], u_s[:, None]], axis=1)[:, -(CONV_WIDTH - 1):])
        outs["ps"].append(jnp.concatenate([state_pool[l], col(c_ab + 3 * cc, pw, sr)[:, None]], axis=1)[:, -POOL_STATE:])

    y_prompt = x[:t_p].reshape(n_b, seq, d)
    y_sample = x[t_p:t_p + n_s].reshape(n_s, dec_seq, d)
    stk = {k: jnp.stack(v) for k, v in outs.items()}
    return (y_prompt, y_sample, stk["kp"], stk["vp"], stk["kip"], stk["cp"], stk["pp"],
            stk["ks"], stk["vs"], stk["kis"], stk["cs"], stk["ps"])
```
